```python
import math
import jax, jax.numpy as jnp
from jax import lax
import numpy as np

D_MODEL = 1024
BATCH = 4
SEQ = 4096
DEPTH = 2

GRID_W = 64
CTX_LEN = 256
Q_BLOCK = 128
ROPE_THETA = 10000.0
EPS = 1e-6

A_HEADS = 8
A_KV_HEADS = 2
A_GROUP = A_HEADS // A_KV_HEADS
A_HEAD_DIM = 64
A_WIDTH = A_HEADS * A_HEAD_DIM

S5_CH = 512
S5_GROUP_CH = 16
S5_GROUPS = S5_CH // S5_GROUP_CH
S5_STATE = 64

C_HEADS = 8
C_NOPE = 64
C_ROPE = 32
C_VDIM = 64
C_Q_RANK = 768
C_KV_RANK = 256
C_QK_DIM = C_NOPE + C_ROPE
C_WIDTH = C_HEADS * C_VDIM

D_FF = 4 * D_MODEL
N_BRANCH = 3
N_MOD = 6
DEEPNORM_ALPHA = (2.0 * DEPTH) ** 0.25
DEEPNORM_BETA = (8.0 * DEPTH) ** -0.25

OFF_AK = 0
OFF_AV = OFF_AK + A_KV_HEADS * A_HEAD_DIM
OFF_CKV = OFF_AV + A_KV_HEADS * A_HEAD_DIM
OFF_CKR = OFF_CKV + C_KV_RANK
OFF_U = OFF_CKR + C_ROPE
N_STATE_COLS = OFF_U + S5_CH
OFF_AQ = N_STATE_COLS
OFF_CQ = OFF_AQ + A_WIDTH
OFF_GATE = OFF_CQ + C_Q_RANK
N_IN_COLS = OFF_GATE + N_BRANCH * D_MODEL

kernel_name = 'hybrid_gqa_s5_mla_dit_block'


def layer_norm(x, g=None, b=None):
    x32 = x.astype(jnp.float32)
    mu = jnp.mean(x32, axis=-1, keepdims=True)
    var = jnp.mean(jnp.square(x32 - mu), axis=-1, keepdims=True)
    y = (x32 - mu) * lax.rsqrt(var + EPS)
    if g is not None:
        y = y * g.astype(jnp.float32) + b.astype(jnp.float32)
    return y.astype(x.dtype)


def rms_norm(x, g):
    x32 = x.astype(jnp.float32)
    y = x32 * lax.rsqrt(jnp.mean(jnp.square(x32), axis=-1, keepdims=True) + EPS)
    return (y * g.astype(jnp.float32)).astype(x.dtype)


def modulate(x, shift, scale):
    return layer_norm(x) * (1.0 + scale) + shift


def post_norm(x, y, g, b):
    return layer_norm(DEEPNORM_ALPHA * x + y, g, b)


def axial_rope_tables(rows, dim):
    half = dim // 2
    inv = ROPE_THETA ** (-jnp.arange(0, half, 2, dtype=jnp.float32) / half)
    row = jnp.repeat(jnp.arange(rows, dtype=jnp.float32), GRID_W)
    col = jnp.tile(jnp.arange(GRID_W, dtype=jnp.float32), rows)
    ang_r = row[:, None] * inv
    ang_c = col[:, None] * inv
    ang = jnp.concatenate([ang_r, ang_r, ang_c, ang_c], axis=-1)
    return jnp.cos(ang), jnp.sin(ang)


def apply_rope(x, cos, sin):
    half = x.shape[-1] // 2
    q = half // 2

    def rot(v):
        return jnp.concatenate([-v[..., q:], v[..., :q]], axis=-1)

    xr = jnp.concatenate([rot(x[..., :half]), rot(x[..., half:])], axis=-1)
    out = x.astype(jnp.float32) * cos[:, None, :] + xr.astype(jnp.float32) * sin[:, None, :]
    return out.astype(x.dtype)


def block_attention(q, k, v, scale):
    bsz, kvh, grp, lq, dk = q.shape
    nb = lq // Q_BLOCK
    qb = jnp.moveaxis(q.reshape(bsz, kvh, grp, nb, Q_BLOCK, dk), 3, 0)

    def one_block(qblk):
        s = jnp.einsum('bhgqd,bhkd->bhgqk', qblk, k, preferred_element_type=jnp.float32) * scale
        p = jax.nn.softmax(s, axis=-1).astype(v.dtype)
        return jnp.einsum('bhgqk,bhkd->bhgqd', p, v)

    o = lax.map(one_block, qb)
    return jnp.moveaxis(o, 0, 3).reshape(bsz, kvh, grp, lq, v.shape[-1])


def gqa_kv(proj, k_gain, rope):
    bsz, L = proj.shape[:2]
    k = rms_norm(proj[..., OFF_AK:OFF_AV].reshape(bsz, L, A_KV_HEADS, A_HEAD_DIM), k_gain)
    v = proj[..., OFF_AV:OFF_CKV].reshape(bsz, L, A_KV_HEADS, A_HEAD_DIM)
    if rope is not None:
        k = apply_rope(k, *rope)
    return k.transpose(0, 2, 1, 3), v.transpose(0, 2, 1, 3)


def mla_kv(proj, kv_a_gain, w_kvb, rope):
    bsz, L = proj.shape[:2]
    c_kv = rms_norm(proj[..., OFF_CKV:OFF_CKR], kv_a_gain)
    kv = (c_kv @ w_kvb).reshape(bsz, L, C_HEADS, C_NOPE + C_VDIM)
    k_nope, v = kv[..., :C_NOPE], kv[..., C_NOPE:]
    k_rope = proj[..., OFF_CKR:OFF_U][:, :, None, :]
    if rope is not None:
        k_rope = apply_rope(k_rope, *rope)
    k = jnp.concatenate([k_nope, jnp.broadcast_to(k_rope, (bsz, L, C_HEADS, C_ROPE))], axis=-1)
    return k.transpose(0, 2, 1, 3), v.transpose(0, 2, 1, 3)


def attend_queries(proj, ka, va, kc, vc, lp, rope_a, rope_c):
    bsz, L = proj.shape[:2]
    qa = rms_norm(proj[..., OFF_AQ:OFF_CQ].reshape(bsz, L, A_HEADS, A_HEAD_DIM), lp['a_q_gain'])
    if rope_a is not None:
        qa = apply_rope(qa, *rope_a)
    qa = qa.reshape(bsz, L, A_KV_HEADS, A_GROUP, A_HEAD_DIM).transpose(0, 2, 3, 1, 4)
    oa = block_attention(qa, ka, va, A_HEAD_DIM ** -0.5)
    ya = oa.transpose(0, 3, 1, 2, 4).reshape(bsz, L, A_WIDTH)
    cq = rms_norm(proj[..., OFF_CQ:OFF_GATE], lp['c_q_a_gain'])
    qc = (cq @ lp['c_w_qb']).reshape(bsz, L, C_HEADS, C_QK_DIM)
    q_nope, q_rope = qc[..., :C_NOPE], qc[..., C_NOPE:]
    if rope_c is not None:
        q_rope = apply_rope(q_rope, *rope_c)
    qc = jnp.concatenate([q_nope, q_rope], axis=-1).transpose(0, 2, 1, 3)[:, :, None]
    oc = block_attention(qc, kc, vc, C_QK_DIM ** -0.5)
    yc = oc[:, :, 0].transpose(0, 2, 1, 3).reshape(bsz, L, C_WIDTH)
    return ya, yc


def s5_discretize(a_re, a_im, log_dt, b_re, b_im):
    f32 = jnp.float32
    a_re, a_im = a_re.astype(f32), a_im.astype(f32)
    dt = jnp.exp(log_dt.astype(f32))[:, None]
    mag = jnp.exp(a_re * dt)
    abar_r = mag * jnp.cos(a_im * dt)
    abar_i = mag * jnp.sin(a_im * dt)
    den = a_re * a_re + a_im * a_im
    nr = abar_r - 1.0
    coef_r = (nr * a_re + abar_i * a_im) / den
    coef_i = (abar_i * a_re - nr * a_im) / den
    b_re, b_im = b_re.astype(f32), b_im.astype(f32)
    bbar_r = coef_r[..., None] * b_re - coef_i[..., None] * b_im
    bbar_i = coef_r[..., None] * b_im + coef_i[..., None] * b_re
    return abar_r, abar_i, bbar_r, bbar_i


def _complex_affine_combine(earlier, later):
    a1r, a1i, b1r, b1i = earlier
    a2r, a2i, b2r, b2i = later
    return (a1r * a2r - a1i * a2i,
            a1r * a2i + a1i * a2r,
            a2r * b1r - a2i * b1i + b2r,
            a2r * b1i + a2i * b1r + b2i)


def s5_states(u, disc, x0, reverse):
    abar_r, abar_i, bbar_r, bbar_i = disc
    u32 = u.astype(jnp.float32)
    bu_r = jnp.einsum('blgc,gpc->blgp', u32, bbar_r)
    bu_i = jnp.einsum('blgc,gpc->blgp', u32, bbar_i)
    L = u.shape[1]
    ar = jnp.broadcast_to(abar_r, (1, L) + abar_r.shape)
    ai = jnp.broadcast_to(abar_i, (1, L) + abar_i.shape)
    cum_r, cum_i, s_r, s_i = lax.associative_scan(
        _complex_affine_combine, (ar, ai, bu_r, bu_i), axis=1, reverse=reverse)
    if x0 is not None:
        x0r, x0i = x0[0][:, None], x0[1][:, None]
        s_r = s_r + cum_r * x0r - cum_i * x0i
        s_i = s_i + cum_r * x0i + cum_i * x0r
    return s_r, s_i


def s5_readout(s_r, s_i, c_re, c_im):
    return (jnp.einsum('gcp,blgp->blgc', c_re.astype(jnp.float32), s_r)
            - jnp.einsum('gcp,blgp->blgc', c_im.astype(jnp.float32), s_i))


def s5_input(proj):
    return proj[..., OFF_U:N_STATE_COLS].reshape(proj.shape[0], proj.shape[1], S5_GROUPS, S5_GROUP_CH)


def s5_glu(ys, u, d, w_glu):
    y = ys + d.astype(jnp.float32).reshape(S5_GROUPS, S5_GROUP_CH) * u.astype(jnp.float32)
    y = y.reshape(u.shape[0], u.shape[1], S5_CH).astype(u.dtype)
    h = jax.nn.gelu(y) @ w_glu
    a, g = jnp.split(h, 2, axis=-1)
    return a * jax.nn.sigmoid(g)


def merge_branches(proj, ya, ys, yc, lp):
    bsz, L = proj.shape[:2]
    g = jax.nn.sigmoid(proj[..., OFF_GATE:]).reshape(bsz, L, N_BRANCH, D_MODEL)
    merged = (g[..., 0, :] * (ya @ lp['w_branch_a'])
              + g[..., 1, :] * (ys @ lp['w_branch_s5'])
              + g[..., 2, :] * (yc @ lp['w_branch_c']))
    return merged @ lp['w_out']


def squared_relu_mlp(h, w_up, w_down):
    return jnp.square(jax.nn.relu(h @ w_up)) @ w_down


def setup_inputs(seed: int = 0) -> dict:
    key = jax.random.key(seed)
    ks = iter(jax.random.split(key, 64))

    def nrm(shape, scale):
        return scale * jax.random.normal(next(ks), shape, jnp.float32)

    G, P, CH = S5_GROUPS, S5_STATE, S5_GROUP_CH
    return {
        'x': nrm((BATCH, SEQ, D_MODEL), 1.0),
        'c': nrm((BATCH, D_MODEL), 1.0),
        'ctx': nrm((BATCH, CTX_LEN, D_MODEL), 1.0),
        'c_ctx': nrm((D_MODEL,), 1.0),
        'w_mod': nrm((DEPTH, D_MODEL, N_MOD * D_MODEL), 0.5 * D_MODEL ** -0.5),
        'b_mod': nrm((DEPTH, N_MOD * D_MODEL), 0.01),
        'w_in': nrm((DEPTH, D_MODEL, N_IN_COLS), D_MODEL ** -0.5),
        'a_q_gain': 1.0 + nrm((DEPTH, A_HEAD_DIM), 0.02),
        'a_k_gain': 1.0 + nrm((DEPTH, A_HEAD_DIM), 0.02),
        'c_q_a_gain': 1.0 + nrm((DEPTH, C_Q_RANK), 0.02),
        'c_kv_a_gain': 1.0 + nrm((DEPTH, C_KV_RANK), 0.02),
        'c_w_qb': nrm((DEPTH, C_Q_RANK, C_HEADS * C_QK_DIM), C_Q_RANK ** -0.5),
        'c_w_kvb': nrm((DEPTH, C_KV_RANK, C_HEADS * (C_NOPE + C_VDIM)), C_KV_RANK ** -0.5),
        's5_a_re': -0.5 + nrm((DEPTH, 2, G, P), 0.01),
        's5_a_im': jnp.pi * jnp.arange(P, dtype=jnp.float32) + nrm((DEPTH, 2, G, P), 0.01),
        's5_log_dt': jax.random.uniform(next(ks), (DEPTH, 2, G), jnp.float32, math.log(1e-3), math.log(1e-1)),
        's5_b_re': nrm((DEPTH, 2, G, P, CH), (2.0 * CH) ** -0.5),
        's5_b_im': nrm((DEPTH, 2, G, P, CH), (2.0 * CH) ** -0.5),
        's5_c_re': nrm((DEPTH, 2, G, CH, P), P ** -0.5),
        's5_c_im': nrm((DEPTH, 2, G, CH, P), P ** -0.5),
        's5_d': nrm((DEPTH, S5_CH), 1.0),
        's5_w_glu': nrm((DEPTH, S5_CH, 2 * S5_CH), S5_CH ** -0.5),
        'w_branch_a': nrm((DEPTH, A_WIDTH, D_MODEL), A_WIDTH ** -0.5),
        'w_branch_s5': nrm((DEPTH, S5_CH, D_MODEL), S5_CH ** -0.5),
        'w_branch_c': nrm((DEPTH, C_WIDTH, D_MODEL), C_WIDTH ** -0.5),
        'w_out': nrm((DEPTH, D_MODEL, D_MODEL), DEEPNORM_BETA * D_MODEL ** -0.5),
        'ln1_g': 1.0 + nrm((DEPTH, D_MODEL), 0.02),
        'ln1_b': nrm((DEPTH, D_MODEL), 0.02),
        'w_up': nrm((DEPTH, D_MODEL, D_FF), D_MODEL ** -0.5),
        'w_down': nrm((DEPTH, D_FF, D_MODEL), DEEPNORM_BETA * D_FF ** -0.5),
        'ln2_g': 1.0 + nrm((DEPTH, D_MODEL), 0.02),
        'ln2_b': nrm((DEPTH, D_MODEL), 0.02),
    }


def reference(x, c, ctx, c_ctx, w_mod, b_mod, w_in, a_q_gain, a_k_gain, c_q_a_gain, c_kv_a_gain,
              c_w_qb, c_w_kvb, s5_a_re, s5_a_im, s5_log_dt, s5_b_re, s5_b_im, s5_c_re, s5_c_im,
              s5_d, s5_w_glu, w_branch_a, w_branch_s5, w_branch_c, w_out, ln1_g, ln1_b,
              w_up, w_down, ln2_g, ln2_b):
    rows = x.shape[1] // GRID_W
    rope_a = axial_rope_tables(rows, A_HEAD_DIM)
    rope_c = axial_rope_tables(rows, C_ROPE)
    for l in range(DEPTH):
        last = l == DEPTH - 1
        lp = {'a_q_gain': a_q_gain[l], 'a_k_gain': a_k_gain[l],
              'c_q_a_gain': c_q_a_gain[l], 'c_kv_a_gain': c_kv_a_gain[l],
              'c_w_qb': c_w_qb[l], 'c_w_kvb': c_w_kvb[l],
              'w_branch_a': w_branch_a[l], 'w_branch_s5': w_branch_s5[l],
              'w_branch_c': w_branch_c[l], 'w_out': w_out[l]}
        disc = [s5_discretize(s5_a_re[l, dr], s5_a_im[l, dr], s5_log_dt[l, dr],
                              s5_b_re[l, dr], s5_b_im[l, dr]) for dr in range(2)]

        mod = jax.nn.silu(c) @ w_mod[l] + b_mod[l]
        sh1, sc1, g1, sh2, sc2, g2 = jnp.split(mod[:, None, :], N_MOD, axis=-1)
        n_ctx_mod = 2 if last else N_MOD
        mod_c = jax.nn.silu(c_ctx) @ w_mod[l][:, :n_ctx_mod * D_MODEL] + b_mod[l][:n_ctx_mod * D_MODEL]
        mods_c = jnp.split(mod_c, n_ctx_mod)

        h = modulate(x, sh1, sc1)
        hc = modulate(ctx, mods_c[0], mods_c[1])
        proj = h @ w_in[l]
        proj_c = hc @ (w_in[l][:, :N_STATE_COLS] if last else w_in[l])

        ka_c, va_c = gqa_kv(proj_c, lp['a_k_gain'], None)
        kc_c, vc_c = mla_kv(proj_c, lp['c_kv_a_gain'], lp['c_w_kvb'], None)
        u_c = s5_input(proj_c)
        st_f = s5_states(u_c, disc[0], None, False)
        st_b = s5_states(u_c, disc[1], None, True)

        ka, va = gqa_kv(proj, lp['a_k_gain'], rope_a)
        kc, vc = mla_kv(proj, lp['c_kv_a_gain'], lp['c_w_kvb'], rope_c)
        ya, yc = attend_queries(proj,
                                jnp.concatenate([ka_c, ka], axis=2), jnp.concatenate([va_c, va], axis=2),
                                jnp.concatenate([kc_c, kc], axis=2), jnp.concatenate([vc_c, vc], axis=2),
                                lp, rope_a, rope_c)
        u = s5_input(proj)
        ys = (s5_readout(*s5_states(u, disc[0], (st_f[0][:, -1], st_f[1][:, -1]), False), s5_c_re[l, 0], s5_c_im[l, 0])
              + s5_readout(*s5_states(u, disc[1], (st_b[0][:, 0], st_b[1][:, 0]), True), s5_c_re[l, 1], s5_c_im[l, 1]))
        ys = s5_glu(ys, u, s5_d[l], s5_w_glu[l])
        mix = merge_branches(proj, ya, ys, yc, lp)
        x_mid = post_norm(x, g1 * mix, ln1_g[l], ln1_b[l])
        ff = squared_relu_mlp(modulate(x_mid, sh2, sc2), w_up[l], w_down[l])
        x_next = post_norm(x_mid, g2 * ff, ln2_g[l], ln2_b[l])

        if not last:
            ya_c, yc_c = attend_queries(proj_c, ka_c, va_c, kc_c, vc_c, lp, None, None)
            ys_c = (s5_readout(*st_f, s5_c_re[l, 0], s5_c_im[l, 0])
                    + s5_readout(*st_b, s5_c_re[l, 1], s5_c_im[l, 1]))
            ys_c = s5_glu(ys_c, u_c, s5_d[l], s5_w_glu[l])
            mix_c = merge_branches(proj_c, ya_c, ys_c, yc_c, lp)
            ctx_mid = post_norm(ctx, mods_c[2] * mix_c, ln1_g[l], ln1_b[l])
            ff_c = squared_relu_mlp(modulate(ctx_mid, mods_c[3], mods_c[4]), w_up[l], w_down[l])
            ctx = post_norm(ctx_mid, mods_c[5] * ff_c, ln2_g[l], ln2_b[l])
        x = x_next
    return x
```

```python
import functools
import math

import jax
import jax.numpy as jnp
from jax import lax
from jax.experimental import pallas as pl
from jax.experimental.pallas import tpu as pltpu

F32 = jnp.float32
BF16 = jnp.bfloat16

D_MODEL = 1024
DEPTH = 2
GRID_W = 64
ROPE_THETA = 10000.0
EPS = 1e-6

A_HEADS = 8
A_KV_HEADS = 2
A_HEAD_DIM = 64
A_WIDTH = A_HEADS * A_HEAD_DIM

S5_CH = 512
S5_GROUP_CH = 16
S5_GROUPS = S5_CH // S5_GROUP_CH
S5_STATE = 64
S5_LANES = S5_GROUPS * S5_STATE

C_HEADS = 8
C_NOPE = 64
C_ROPE = 32
C_VDIM = 64
C_Q_RANK = 768
C_KV_RANK = 256
C_QK_DIM = C_NOPE + C_ROPE
C_WIDTH = C_HEADS * C_VDIM

D_FF = 4 * D_MODEL
N_BRANCH = 3
N_MOD = 6
DEEPNORM_ALPHA = (2.0 * DEPTH) ** 0.25

OFF_AK = 0
OFF_AV = OFF_AK + A_KV_HEADS * A_HEAD_DIM
OFF_CKV = OFF_AV + A_KV_HEADS * A_HEAD_DIM
OFF_CKR = OFF_CKV + C_KV_RANK
OFF_U = OFF_CKR + C_ROPE
OFF_AQ = OFF_U + S5_CH
OFF_CQ = OFF_AQ + A_WIDTH
OFF_GATE = OFF_CQ + C_Q_RANK
N_IN_COLS = OFF_GATE + N_BRANCH * D_MODEL

LANE = 128
C_HEAD_PAD = LANE

P_AK = 0
P_AV = P_AK + LANE
P_CKV = P_AV + LANE
P_CKR = P_CKV + C_KV_RANK
P_U = P_CKR + LANE
P_AQ = P_U + S5_CH
P_CQ = P_AQ + A_WIDTH
P_GATE = P_CQ + C_Q_RANK
P_COLS = P_GATE + N_BRANCH * D_MODEL

ROW_BLOCK = 256
ATTN_Q_BLOCK = 256
S5_CHUNK = 256
S5_SUB = 8
S5_SUB_LEN = S5_CHUNK // S5_SUB
S5_LANE_CHUNK = 512
VMEM_LIMIT = 56 * 1024 * 1024


def _const_spec(shape):
    nd = len(shape)
    return pl.BlockSpec(shape, lambda *_: (0,) * nd, pipeline_mode=pl.Buffered(1))


def _params(n_grid):
    return pltpu.CompilerParams(dimension_semantics=("arbitrary",) * n_grid,
                                vmem_limit_bytes=VMEM_LIMIT)


def _layer_norm(x):
    mu = jnp.mean(x, axis=-1, keepdims=True)
    xc = x - mu
    var = jnp.mean(xc * xc, axis=-1, keepdims=True)
    return xc * lax.rsqrt(var + EPS)


def _dot(a, b):
    return jnp.dot(a, b, preferred_element_type=F32)


def _mod_kernel(c_ref, w_ref, b_ref, o_ref):
    s = jax.nn.silu(c_ref[...]).astype(BF16)
    o_ref[...] = _dot(s, w_ref[...].astype(BF16)) + b_ref[...]


def _modulation(c_all, w_mod, b_mod, layer):
    n = N_MOD * D_MODEL
    bn = n // 4
    return pl.pallas_call(
        _mod_kernel,
        grid=(n // bn,),
        in_specs=[pl.BlockSpec((8, D_MODEL), lambda j: (0, 0)),
                  pl.BlockSpec((None, D_MODEL, bn), lambda j: (layer, 0, j)),
                  pl.BlockSpec((None, 1, bn), lambda j: (layer, 0, j))],
        out_specs=pl.BlockSpec((8, bn), lambda j: (0, j)),
        out_shape=jax.ShapeDtypeStruct((8, n), F32),
        compiler_params=_params(1),
        name="mod",
    )(c_all, w_mod, b_mod.reshape(DEPTH, 1, n))


def _rope(x, cos, sin_next, sin_prev, shift):
    n = x.shape[-1]
    return x * cos + pltpu.roll(x, n - shift, 1) * sin_next + pltpu.roll(x, shift, 1) * sin_prev


def _head_rms_norm(p, ones_ref, gain):
    sq = p * p
    hi = sq.astype(BF16)
    lo = (sq - hi.astype(F32)).astype(BF16)
    ms = _dot(hi, ones_ref[...]) + _dot(lo, ones_ref[...])
    return p * lax.rsqrt(ms + EPS) * gain


def _row_rms_norm(p, gain):
    return p * lax.rsqrt(jnp.mean(p * p, axis=-1, keepdims=True) + EPS) * gain


def _inproj_kernel(x_ref, mod_ref, w_ref, wkc_ref, wvc_ref, wqb_ref, ones_ref,
                   gk_ref, gq_ref, gkv_ref, gcq_ref,
                   cosa_ref, sna_ref, spa_ref, cosc_ref, snc_ref, spc_ref,
                   ka_ref, va_ref, kc_ref, vc_ref, u_ref, qa_ref, qc_ref, gate_ref, *, rope):
    mod = mod_ref[0]
    h = (_layer_norm(x_ref[0]) * (1.0 + mod[1:2]) + mod[0:1]).astype(BF16)

    def proj(a, b):
        return _dot(h, w_ref[:, a:b])

    def rope_a(t):
        return _rope(t, cosa_ref[...], sna_ref[...], spa_ref[...], A_HEAD_DIM // 4) if rope else t

    def rope_c(t):
        return _rope(t, cosc_ref[...], snc_ref[...], spc_ref[...], C_ROPE // 4) if rope else t

    k = rope_a(_head_rms_norm(proj(P_AK, P_AV), ones_ref, gk_ref[...])).astype(BF16)
    v = proj(P_AV, P_CKV).astype(BF16)
    for hh in range(A_KV_HEADS):
        ka_ref[0, hh] = k[:, hh * A_HEAD_DIM:(hh + 1) * A_HEAD_DIM]
        va_ref[0, hh] = v[:, hh * A_HEAD_DIM:(hh + 1) * A_HEAD_DIM]

    ckv = _row_rms_norm(proj(P_CKV, P_CKR), gkv_ref[...]).astype(BF16)
    k_rope = rope_c(proj(P_CKR, P_U))
    k_nope = _dot(ckv, wkc_ref[...])
    vc = _dot(ckv, wvc_ref[...]).astype(BF16)
    for hh in range(C_HEADS):
        kc_ref[0, hh] = (k_nope[:, hh * C_HEAD_PAD:(hh + 1) * C_HEAD_PAD] + k_rope).astype(BF16)
        vc_ref[0, hh] = vc[:, hh * C_VDIM:(hh + 1) * C_VDIM]

    u_ref[0] = proj(P_U, P_AQ)

    for t in range(A_WIDTH // LANE):
        q = _head_rms_norm(proj(P_AQ + t * LANE, P_AQ + (t + 1) * LANE), ones_ref, gq_ref[...])
        qa_ref[0, :, t * LANE:(t + 1) * LANE] = rope_a(q).astype(BF16)

    cq = _row_rms_norm(proj(P_CQ, P_GATE), gcq_ref[...]).astype(BF16)
    for hh in range(C_HEADS):
        q = rope_c(_dot(cq, wqb_ref[:, hh * C_HEAD_PAD:(hh + 1) * C_HEAD_PAD]))
        qc_ref[0, :, hh * C_HEAD_PAD:(hh + 1) * C_HEAD_PAD] = (q * (C_QK_DIM ** -0.5)).astype(BF16)

    for t in range(N_BRANCH):
        a = P_GATE + t * D_MODEL
        gate_ref[0, :, t * D_MODEL:(t + 1) * D_MODEL] = jax.nn.sigmoid(proj(a, a + D_MODEL))


def _inproj(x, mod, lw, tabs, rope):
    bsz, n_tok, _ = x.shape
    nb = n_tok // ROW_BLOCK
    per_batch_mod = mod.shape[0] > 1

    def row_spec(width):
        return pl.BlockSpec((1, ROW_BLOCK, width), lambda b, i: (b, i, 0))

    def head_spec(heads, width):
        return pl.BlockSpec((1, heads, ROW_BLOCK, width), lambda b, i: (b, 0, i, 0))

    tab_spec = pl.BlockSpec((ROW_BLOCK, LANE), lambda b, i: (i, 0))
    mod_spec = pl.BlockSpec((1, N_MOD, D_MODEL), (lambda b, i: (b, 0, 0)) if per_batch_mod else (lambda b, i: (0, 0, 0)))
    consts = [lw['w_in'], lw['w_kc'], lw['w_vc'], lw['w_qb'], lw['ones'],
              lw['gk'], lw['gq'], lw['gkv'], lw['gcq']]
    out_shape = [
        jax.ShapeDtypeStruct((bsz, A_KV_HEADS, n_tok, A_HEAD_DIM), BF16),
        jax.ShapeDtypeStruct((bsz, A_KV_HEADS, n_tok, A_HEAD_DIM), BF16),
        jax.ShapeDtypeStruct((bsz, C_HEADS, n_tok, C_HEAD_PAD), BF16),
        jax.ShapeDtypeStruct((bsz, C_HEADS, n_tok, C_VDIM), BF16),
        jax.ShapeDtypeStruct((bsz, n_tok, S5_CH), F32),
        jax.ShapeDtypeStruct((bsz, n_tok, A_WIDTH), BF16),
        jax.ShapeDtypeStruct((bsz, n_tok, C_HEADS * C_HEAD_PAD), BF16),
        jax.ShapeDtypeStruct((bsz, n_tok, N_BRANCH * D_MODEL), F32),
    ]
    out_specs = [head_spec(A_KV_HEADS, A_HEAD_DIM), head_spec(A_KV_HEADS, A_HEAD_DIM),
                 head_spec(C_HEADS, C_HEAD_PAD), head_spec(C_HEADS, C_VDIM),
                 row_spec(S5_CH), row_spec(A_WIDTH), row_spec(C_HEADS * C_HEAD_PAD),
                 row_spec(N_BRANCH * D_MODEL)]
    return pl.pallas_call(
        functools.partial(_inproj_kernel, rope=rope),
        grid=(bsz, nb),
        in_specs=[row_spec(D_MODEL), mod_spec] + [_const_spec(a.shape) for a in consts] + [tab_spec] * 6,
        out_specs=out_specs,
        out_shape=out_shape,
        compiler_params=_params(2),
        name="inproj",
    )(x, mod, *consts, *tabs)


def _softmax_attend(q, ks, vs):
    ss = [lax.dot_general(q, k, (((1,), (1,)), ((), ())), preferred_element_type=F32) for k in ks]
    m = ss[0].max(axis=-1, keepdims=True)
    for s in ss[1:]:
        m = jnp.maximum(m, s.max(axis=-1, keepdims=True))
    denom = None
    out = None
    for s, v in zip(ss, vs):
        p = jnp.exp(s - m)
        d = p.sum(axis=-1, keepdims=True)
        o = _dot(p.astype(BF16), v)
        denom = d if denom is None else denom + d
        out = o if out is None else out + o
    return out / denom


def _attn_kernel(q_ref, *refs, n_seg, shared_kv, dk, dv):
    kv_refs, o_ref = refs[:2 * n_seg], refs[2 * n_seg]
    tq = q_ref.shape[1]
    if shared_kv:
        q = jnp.concatenate([q_ref[0, :, :dk], q_ref[0, :, dk:]], axis=0)
        o = _softmax_attend(q, [kv_refs[2 * s][0, 0] for s in range(n_seg)],
                            [kv_refs[2 * s + 1][0, 0] for s in range(n_seg)]).astype(BF16)
        o_ref[0, :, :dv] = o[:tq]
        o_ref[0, :, dv:] = o[tq:]
    else:
        for hh in range(2):
            o = _softmax_attend(q_ref[0, :, hh * dk:(hh + 1) * dk],
                                [kv_refs[2 * s][0, hh] for s in range(n_seg)],
                                [kv_refs[2 * s + 1][0, hh] for s in range(n_seg)])
            o_ref[0, :, hh * dv:(hh + 1) * dv] = o.astype(BF16)


def _attention(q, kvs, *, heads, kv_heads, dk, dv, name):
    bsz, lq, _ = q.shape
    tq = min(ATTN_Q_BLOCK, lq)
    shared_kv = kv_heads < heads
    group = heads // kv_heads
    n_pairs = heads // 2
    in_specs = [pl.BlockSpec((1, tq, 2 * dk), lambda b, p, i: (b, i, p))]
    args = [q]
    for k, v in kvs:
        lk = k.shape[2]
        if shared_kv:
            kmap = lambda b, p, i: (b, (2 * p) // group, 0, 0)
            in_specs += [pl.BlockSpec((1, 1, lk, dk), kmap), pl.BlockSpec((1, 1, lk, dv), kmap)]
        else:
            kmap = lambda b, p, i: (b, p, 0, 0)
            in_specs += [pl.BlockSpec((1, 2, lk, dk), kmap), pl.BlockSpec((1, 2, lk, dv), kmap)]
        args += [k, v]
    return pl.pallas_call(
        functools.partial(_attn_kernel, n_seg=len(kvs), shared_kv=shared_kv, dk=dk, dv=dv),
        grid=(bsz, n_pairs, lq // tq),
        in_specs=in_specs,
        out_specs=pl.BlockSpec((1, tq, 2 * dv), lambda b, p, i: (b, i, p)),
        out_shape=jax.ShapeDtypeStruct((bsz, lq, heads * dv), BF16),
        compiler_params=_params(3),
        name=name,
    )(*args)


def _s5_kernel(u_ref, cin_ref, wb_ref, a_ref, apow_ref, wc_ref, yprev_ref, d_ref, wglu_ref,
               out_ref, cout_ref, bu_r, bu_i, xb, car, *, reverse, glu):
    @pl.when(pl.program_id(1) == 0)
    def _():
        car[...] = cin_ref[0]

    ub = u_ref[0].astype(BF16)
    n_tiles = S5_CH // LANE
    tile_states = S5_LANES // n_tiles
    per_tile = tile_states // LANE
    for t in range(n_tiles):
        r = _dot(ub[:, t * LANE:(t + 1) * LANE], wb_ref[t])
        for k in range(per_tile):
            bu_r[t * per_tile + k] = r[:, k * LANE:(k + 1) * LANE]
            bu_i[t * per_tile + k] = r[:, tile_states + k * LANE:tile_states + (k + 1) * LANE]

    steps = range(S5_SUB_LEN - 1, -1, -1) if reverse else range(S5_SUB_LEN)
    subs = range(S5_SUB - 1, -1, -1) if reverse else range(S5_SUB)
    for c in range(S5_LANES // S5_LANE_CHUNK):
        tiles = range(c * S5_LANE_CHUNK // LANE, (c + 1) * S5_LANE_CHUNK // LANE)
        lanes = [slice(q * LANE, (q + 1) * LANE) for q in tiles]
        ar = [jnp.broadcast_to(a_ref[0:1, ls], (S5_SUB, LANE)) for ls in lanes]
        ai = [jnp.broadcast_to(a_ref[1:2, ls], (S5_SUB, LANE)) for ls in lanes]
        xr = [jnp.zeros((S5_SUB, LANE), F32) for _ in tiles]
        xi = [jnp.zeros((S5_SUB, LANE), F32) for _ in tiles]
        for i in steps:
            rows = pl.ds(i, S5_SUB, stride=S5_SUB_LEN)
            for k, q in enumerate(tiles):
                xr[k], xi[k] = (ar[k] * xr[k] - ai[k] * xi[k] + bu_r[q, rows, :],
                                ar[k] * xi[k] + ai[k] * xr[k] + bu_i[q, rows, :])
                bu_r[q, rows, :] = xr[k]
                bu_i[q, rows, :] = xi[k]
        for k, q in enumerate(tiles):
            ls = lanes[k]
            cr, ci = car[0:1, ls], car[1:2, ls]
            a_sub_r, a_sub_i = a_ref[2:3, ls], a_ref[3:4, ls]
            pr, pi = apow_ref[0, :, ls], apow_ref[1, :, ls]
            col = (q // per_tile) * 2 * tile_states + (q % per_tile) * LANE
            for j in subs:
                rows = slice(j * S5_SUB_LEN, (j + 1) * S5_SUB_LEN)
                xb[rows, col:col + LANE] = (bu_r[q, rows, :] + (pr * cr - pi * ci)).astype(BF16)
                xb[rows, col + tile_states:col + tile_states + LANE] = (
                    bu_i[q, rows, :] + (pr * ci + pi * cr)).astype(BF16)
                cr, ci = (a_sub_r * cr - a_sub_i * ci + xr[k][j:j + 1],
                          a_sub_r * ci + a_sub_i * cr + xi[k][j:j + 1])
            car[0:1, ls] = cr
            car[1:2, ls] = ci
    cout_ref[0] = car[...]

    ys = [_dot(xb[:, t * 2 * tile_states:(t + 1) * 2 * tile_states], wc_ref[t]) for t in range(n_tiles)]
    y = jnp.concatenate(ys, axis=-1)
    if glu:
        y = y + yprev_ref[0] + d_ref[...] * u_ref[0]
        hg = _dot(jax.nn.gelu(y).astype(BF16), wglu_ref[...])
        out_ref[0] = (hg[:, :S5_CH] * jax.nn.sigmoid(hg[:, S5_CH:])).astype(out_ref.dtype)
    else:
        out_ref[0] = y


def _s5_scan(u, carry_in, sw, yprev, d, w_glu, *, reverse, glu):
    bsz, n_tok, _ = u.shape
    nc = n_tok // S5_CHUNK
    order = (lambda b, i: (b, nc - 1 - i, 0)) if reverse else (lambda b, i: (b, i, 0))
    tile_states = S5_LANES // (S5_CH // LANE)
    assert S5_LANE_CHUNK == tile_states
    row_spec = pl.BlockSpec((1, S5_CHUNK, S5_CH), order)
    carry_spec = pl.BlockSpec((1, 2, S5_LANES), lambda b, i: (b, 0, 0))
    consts = [sw['wb'], sw['a'], sw['apow'], sw['wc']]
    out, carry = pl.pallas_call(
        functools.partial(_s5_kernel, reverse=reverse, glu=glu),
        grid=(bsz, nc),
        in_specs=[row_spec, carry_spec] + [_const_spec(a.shape) for a in consts]
                 + [row_spec, _const_spec(d.shape), _const_spec(w_glu.shape)],
        out_specs=[row_spec, carry_spec],
        out_shape=[jax.ShapeDtypeStruct((bsz, n_tok, S5_CH), BF16 if glu else F32),
                   jax.ShapeDtypeStruct((bsz, 2, S5_LANES), F32)],
        scratch_shapes=[pltpu.VMEM((S5_LANES // LANE, S5_CHUNK, LANE), F32),
                        pltpu.VMEM((S5_LANES // LANE, S5_CHUNK, LANE), F32),
                        pltpu.VMEM((S5_CHUNK, 2 * S5_LANES), BF16), pltpu.VMEM((2, S5_LANES), F32)],
        compiler_params=_params(2),
        name="s5_bwd" if reverse else "s5_fwd",
    )(u, carry_in, *consts, yprev, d, w_glu)
    return out, carry


def _merge_kernel(x_ref, mod_ref, ya_ref, ys_ref, yc_ref, gate_ref, wa_ref, ws_ref, wc_ref, wo_ref,
                  g_ref, b_ref, o_ref):
    gate = gate_ref[0]
    merged = (gate[:, :D_MODEL] * _dot(ya_ref[0], wa_ref[...])
              + gate[:, D_MODEL:2 * D_MODEL] * _dot(ys_ref[0], ws_ref[...])
              + gate[:, 2 * D_MODEL:] * _dot(yc_ref[0], wc_ref[...]))
    mix = _dot(merged.astype(BF16), wo_ref[...])
    y = DEEPNORM_ALPHA * x_ref[0] + mod_ref[0][2:3] * mix
    o_ref[0] = _layer_norm(y) * g_ref[...] + b_ref[...]


def _mlp_kernel(x_ref, mod_ref, wu_ref, wd_ref, g_ref, b_ref, o_ref):
    x = x_ref[0]
    mod = mod_ref[0]
    h = (_layer_norm(x) * (1.0 + mod[4:5]) + mod[3:4]).astype(BF16)
    up = jnp.square(jnp.maximum(_dot(h, wu_ref[...]), 0.0)).astype(BF16)
    y = DEEPNORM_ALPHA * x + mod[5:6] * _dot(up, wd_ref[...])
    o_ref[0] = _layer_norm(y) * g_ref[...] + b_ref[...]


def _token_call(kernel, name, x, mod, rows, consts):
    bsz, n_tok, _ = x.shape
    per_batch_mod = mod.shape[0] > 1

    def row_spec(width):
        return pl.BlockSpec((1, ROW_BLOCK, width), lambda b, i: (b, i, 0))

    mod_spec = pl.BlockSpec((1, N_MOD, D_MODEL), (lambda b, i: (b, 0, 0)) if per_batch_mod else (lambda b, i: (0, 0, 0)))
    return pl.pallas_call(
        kernel,
        grid=(bsz, n_tok // ROW_BLOCK),
        in_specs=[row_spec(D_MODEL), mod_spec] + [row_spec(r.shape[-1]) for r in rows]
                 + [_const_spec(a.shape) for a in consts],
        out_specs=row_spec(D_MODEL),
        out_shape=jax.ShapeDtypeStruct((bsz, n_tok, D_MODEL), F32),
        compiler_params=_params(2),
        name=name,
    )(x, mod, *rows, *consts)


def _axial_rope_tables(rows, dim):
    half = dim // 2
    inv = ROPE_THETA ** (-jnp.arange(0, half, 2, dtype=F32) / half)
    row = jnp.repeat(jnp.arange(rows, dtype=F32), GRID_W)
    col = jnp.tile(jnp.arange(GRID_W, dtype=F32), rows)
    ang_r = row[:, None] * inv
    ang_c = col[:, None] * inv
    ang = jnp.concatenate([ang_r, ang_r, ang_c, ang_c], axis=-1)
    return jnp.cos(ang), jnp.sin(ang)


def _rope_lane_tables(n_tok, dim, lane_off):
    cos, sin = _axial_rope_tables(n_tok // GRID_W, dim)
    reps = (LANE - lane_off) // dim if lane_off == 0 else 1
    cos_t = jnp.ones((n_tok, LANE), F32).at[:, lane_off:lane_off + reps * dim].set(jnp.tile(cos, (1, reps)))
    sin_t = jnp.zeros((n_tok, LANE), F32).at[:, lane_off:lane_off + reps * dim].set(jnp.tile(sin, (1, reps)))
    quarter = dim // 4
    first = (jnp.arange(LANE) % (2 * quarter)) < quarter
    return cos_t, jnp.where(first, -sin_t, 0.0), jnp.where(first, 0.0, sin_t)


def _block_diag(blocks):
    n, r, c = blocks.shape
    eye = jnp.eye(n, dtype=blocks.dtype)
    return (eye[:, None, :, None] * blocks[:, :, None, :]).reshape(n * r, n * c)


def _s5_discretize(a_re, a_im, log_dt, b_re, b_im):
    dt = jnp.exp(log_dt)[:, None]
    mag = jnp.exp(a_re * dt)
    abar_r = mag * jnp.cos(a_im * dt)
    abar_i = mag * jnp.sin(a_im * dt)
    den = a_re * a_re + a_im * a_im
    nr = abar_r - 1.0
    coef_r = (nr * a_re + abar_i * a_im) / den
    coef_i = (abar_i * a_re - nr * a_im) / den
    bbar_r = coef_r[..., None] * b_re - coef_i[..., None] * b_im
    bbar_i = coef_r[..., None] * b_im + coef_i[..., None] * b_re
    return abar_r, abar_i, bbar_r, bbar_i


def _s5_weights(a_re, a_im, log_dt, b_re, b_im, c_re, c_im, reverse):
    abar_r, abar_i, bbar_r, bbar_i = _s5_discretize(a_re, a_im, log_dt, b_re, b_im)
    groups_per_tile = LANE // S5_GROUP_CH
    n_tiles = S5_GROUPS // groups_per_tile

    def in_tile(bbar):
        blk = jnp.swapaxes(bbar, 1, 2).reshape(n_tiles, groups_per_tile, S5_GROUP_CH, S5_STATE)
        return jax.vmap(_block_diag)(blk)

    def out_tile(cm):
        blk = jnp.swapaxes(cm, 1, 2).reshape(n_tiles, groups_per_tile, S5_STATE, S5_GROUP_CH)
        return jax.vmap(_block_diag)(blk)

    wb = jnp.concatenate([in_tile(bbar_r), in_tile(bbar_i)], axis=-1).astype(BF16)
    wc = jnp.concatenate([out_tile(c_re), -out_tile(c_im)], axis=1).astype(BF16)

    def step(carry, _):
        pr, pi = carry
        nxt = (pr * abar_r - pi * abar_i, pr * abar_i + pi * abar_r)
        return nxt, nxt
    _, (pows_r, pows_i) = lax.scan(step, (jnp.ones_like(abar_r), jnp.zeros_like(abar_r)), None, length=S5_SUB_LEN)
    pows_r = pows_r.reshape(S5_SUB_LEN, S5_LANES)
    pows_i = pows_i.reshape(S5_SUB_LEN, S5_LANES)
    if reverse:
        pows_r, pows_i = pows_r[::-1], pows_i[::-1]
    sub_r = pows_r[0] if reverse else pows_r[-1]
    sub_i = pows_i[0] if reverse else pows_i[-1]
    a = jnp.stack([abar_r.reshape(-1), abar_i.reshape(-1), sub_r, sub_i])
    return {'wb': wb, 'a': a, 'apow': jnp.stack([pows_r, pows_i]), 'wc': wc}


def _layer_weights(l, w_in, a_q_gain, a_k_gain, c_q_a_gain, c_kv_a_gain, c_w_qb, c_w_kvb):
    w = w_in[l]
    ckr = jnp.zeros((D_MODEL, LANE), F32).at[:, C_NOPE:C_QK_DIM].set(w[:, OFF_CKR:OFF_U])
    w_re = jnp.concatenate([w[:, OFF_AK:OFF_CKR], ckr, w[:, OFF_U:]], axis=1).astype(BF16)
    assert w_re.shape[1] == P_COLS
    qb = c_w_qb[l].reshape(C_Q_RANK, C_HEADS, C_QK_DIM)
    qb = jnp.pad(qb, ((0, 0), (0, 0), (0, C_HEAD_PAD - C_QK_DIM))).reshape(C_Q_RANK, C_HEADS * C_HEAD_PAD)
    kvb = c_w_kvb[l].reshape(C_KV_RANK, C_HEADS, C_NOPE + C_VDIM)
    w_kc = jnp.pad(kvb[:, :, :C_NOPE], ((0, 0), (0, 0), (0, C_HEAD_PAD - C_NOPE)))
    w_kc = w_kc.reshape(C_KV_RANK, C_HEADS * C_HEAD_PAD)
    w_vc = kvb[:, :, C_NOPE:].reshape(C_KV_RANK, C_HEADS * C_VDIM)
    heads_per_tile = LANE // A_HEAD_DIM
    ones = _block_diag(jnp.full((heads_per_tile, A_HEAD_DIM, A_HEAD_DIM), 1.0 / A_HEAD_DIM, F32))
    return {
        'w_in': w_re, 'w_kc': w_kc.astype(BF16), 'w_vc': w_vc.astype(BF16), 'w_qb': qb.astype(BF16),
        'ones': ones.astype(BF16),
        'gk': jnp.tile(a_k_gain[l], heads_per_tile)[None],
        'gq': jnp.tile(a_q_gain[l], heads_per_tile)[None] * (A_HEAD_DIM ** -0.5),
        'gkv': c_kv_a_gain[l][None], 'gcq': c_q_a_gain[l][None],
    }


def kernel(x, c, ctx, c_ctx, w_mod, b_mod, w_in, a_q_gain, a_k_gain, c_q_a_gain, c_kv_a_gain, c_w_qb, c_w_kvb, s5_a_re, s5_a_im, s5_log_dt, s5_b_re, s5_b_im, s5_c_re, s5_c_im, s5_d, s5_w_glu, w_branch_a, w_branch_s5, w_branch_c, w_out, ln1_g, ln1_b, w_up, w_down, ln2_g, ln2_b):
    bsz, seq, _ = x.shape
    n_ctx = ctx.shape[1]
    tabs_lat = _rope_lane_tables(seq, A_HEAD_DIM, 0) + _rope_lane_tables(seq, C_ROPE, C_NOPE)
    c_all = jnp.zeros((8, D_MODEL), F32).at[:bsz].set(c).at[bsz].set(c_ctx)
    zero_carry = jnp.zeros((bsz, 2, S5_LANES), F32)

    for l in range(DEPTH):
        last = l == DEPTH - 1
        lw = _layer_weights(l, w_in, a_q_gain, a_k_gain, c_q_a_gain, c_kv_a_gain, c_w_qb, c_w_kvb)
        sw = [_s5_weights(s5_a_re[l, dr], s5_a_im[l, dr], s5_log_dt[l, dr], s5_b_re[l, dr], s5_b_im[l, dr],
                          s5_c_re[l, dr], s5_c_im[l, dr], dr == 1) for dr in range(2)]
        d_row = s5_d[l][None]
        w_glu = s5_w_glu[l].astype(BF16)
        merge_w = [w_branch_a[l].astype(BF16), w_branch_s5[l].astype(BF16), w_branch_c[l].astype(BF16),
                   w_out[l].astype(BF16), ln1_g[l][None], ln1_b[l][None]]
        mlp_w = [w_up[l].astype(BF16), w_down[l].astype(BF16), ln2_g[l][None], ln2_b[l][None]]

        mod = _modulation(c_all, w_mod, b_mod, l).reshape(8, N_MOD, D_MODEL)
        mod_lat, mod_ctx = mod[:bsz], mod[bsz:bsz + 1]

        ka_c, va_c, kc_c, vc_c, u_c, qa_c, qc_c, gate_c = _inproj(ctx, mod_ctx, lw, tabs_lat, rope=False)
        ka, va, kc, vc, u, qa, qc, gate = _inproj(x, mod_lat, lw, tabs_lat, rope=True)

        yf_c, carry_f = _s5_scan(u_c, zero_carry, sw[0], u_c, d_row, w_glu, reverse=False, glu=False)
        yf, _ = _s5_scan(u, carry_f, sw[0], u, d_row, w_glu, reverse=False, glu=False)
        ys_c, carry_b = _s5_scan(u_c, zero_carry, sw[1], yf_c, d_row, w_glu, reverse=True, glu=True)
        ys, _ = _s5_scan(u, carry_b, sw[1], yf, d_row, w_glu, reverse=True, glu=True)

        ya = _attention(qa, [(ka, va), (ka_c, va_c)], heads=A_HEADS, kv_heads=A_KV_HEADS,
                        dk=A_HEAD_DIM, dv=A_HEAD_DIM, name="attn_a")
        yc = _attention(qc, [(kc, vc), (kc_c, vc_c)], heads=C_HEADS, kv_heads=C_HEADS,
                        dk=C_HEAD_PAD, dv=C_VDIM, name="attn_c")
        x_mid = _token_call(_merge_kernel, "merge", x, mod_lat, [ya, ys, yc, gate], merge_w)
        x_next = _token_call(_mlp_kernel, "mlp", x_mid, mod_lat, [], mlp_w)

        if not last:
            ya_c = _attention(qa_c, [(ka_c, va_c)], heads=A_HEADS, kv_heads=A_KV_HEADS,
                              dk=A_HEAD_DIM, dv=A_HEAD_DIM, name="attn_a_ctx")
            yc_c = _attention(qc_c, [(kc_c, vc_c)], heads=C_HEADS, kv_heads=C_HEADS,
                              dk=C_HEAD_PAD, dv=C_VDIM, name="attn_c_ctx")
            ctx_mid = _token_call(_merge_kernel, "merge_ctx", ctx, mod_ctx, [ya_c, ys_c, yc_c, gate_c], merge_w)
            ctx = _token_call(_mlp_kernel, "mlp_ctx", ctx_mid, mod_ctx, [], mlp_w)
        x = x_next
    return x
```

```python
import functools
import math

import jax
import jax.numpy as jnp
from jax import lax
from jax.experimental import pallas as pl
from jax.experimental.pallas import tpu as pltpu

F32 = jnp.float32
BF16 = jnp.bfloat16

D_MODEL = 1024
DEPTH = 2
GRID_W = 64
ROPE_THETA = 10000.0
EPS = 1e-6

A_HEADS = 8
A_KV_HEADS = 2
A_HEAD_DIM = 64
A_WIDTH = A_HEADS * A_HEAD_DIM

S5_CH = 512
S5_GROUP_CH = 16
S5_GROUPS = S5_CH // S5_GROUP_CH
S5_STATE = 64
S5_LANES = S5_GROUPS * S5_STATE

C_HEADS = 8
C_NOPE = 64
C_ROPE = 32
C_VDIM = 64
C_Q_RANK = 768
C_KV_RANK = 256
C_QK_DIM = C_NOPE + C_ROPE
C_WIDTH = C_HEADS * C_VDIM

D_FF = 4 * D_MODEL
N_BRANCH = 3
N_MOD = 6
DEEPNORM_ALPHA = (2.0 * DEPTH) ** 0.25

OFF_AK = 0
OFF_AV = OFF_AK + A_KV_HEADS * A_HEAD_DIM
OFF_CKV = OFF_AV + A_KV_HEADS * A_HEAD_DIM
OFF_CKR = OFF_CKV + C_KV_RANK
OFF_U = OFF_CKR + C_ROPE
OFF_AQ = OFF_U + S5_CH
OFF_CQ = OFF_AQ + A_WIDTH
OFF_GATE = OFF_CQ + C_Q_RANK
N_IN_COLS = OFF_GATE + N_BRANCH * D_MODEL

LANE = 128
C_HEAD_PAD = LANE

P_AK = 0
P_AV = P_AK + LANE
P_CKV = P_AV + LANE
P_CKR = P_CKV + C_KV_RANK
P_U = P_CKR + LANE
P_AQ = P_U + S5_CH
P_CQ = P_AQ + A_WIDTH
P_GATE = P_CQ + C_Q_RANK
P_COLS = P_GATE + N_BRANCH * D_MODEL

ROW_BLOCK = 256
ATTN_TQ = 256
ATTN_KEY_CHUNK = 1024
S5_CHUNK = 256
S5_SUB = 8
S5_SUB_LEN = S5_CHUNK // S5_SUB
S5_LANE_CHUNK = 512
VMEM_LIMIT = 56 * 1024 * 1024


def _const_spec(shape):
    nd = len(shape)
    return pl.BlockSpec(shape, lambda *_: (0,) * nd, pipeline_mode=pl.Buffered(1))


def _params(n_grid):
    return pltpu.CompilerParams(dimension_semantics=("arbitrary",) * n_grid,
                                vmem_limit_bytes=VMEM_LIMIT)


def _layer_norm(x):
    mu = jnp.mean(x, axis=-1, keepdims=True)
    xc = x - mu
    var = jnp.mean(xc * xc, axis=-1, keepdims=True)
    return xc * lax.rsqrt(var + EPS)


def _dot(a, b):
    return jnp.dot(a, b, preferred_element_type=F32)


def _mod_kernel(c_ref, w_ref, b_ref, o_ref):
    s = jax.nn.silu(c_ref[...]).astype(BF16)
    o_ref[...] = _dot(s, w_ref[...].astype(BF16)) + b_ref[...]


def _modulation(c_all, w_mod, b_mod, layer):
    n = N_MOD * D_MODEL
    bn = n // 4
    return pl.pallas_call(
        _mod_kernel,
        grid=(n // bn,),
        in_specs=[pl.BlockSpec((8, D_MODEL), lambda j: (0, 0)),
                  pl.BlockSpec((None, D_MODEL, bn), lambda j: (layer, 0, j)),
                  pl.BlockSpec((None, 1, bn), lambda j: (layer, 0, j))],
        out_specs=pl.BlockSpec((8, bn), lambda j: (0, j)),
        out_shape=jax.ShapeDtypeStruct((8, n), F32),
        compiler_params=_params(1),
        name="mod",
    )(c_all, w_mod, b_mod.reshape(DEPTH, 1, n))


def _rope(x, cos, sin_next, sin_prev, shift):
    n = x.shape[-1]
    return x * cos + pltpu.roll(x, n - shift, 1) * sin_next + pltpu.roll(x, shift, 1) * sin_prev


def _head_rms_norm(p, ones_ref, gain):
    sq = p * p
    hi = sq.astype(BF16)
    lo = (sq - hi.astype(F32)).astype(BF16)
    ms = _dot(hi, ones_ref[...]) + _dot(lo, ones_ref[...])
    return p * lax.rsqrt(ms + EPS) * gain


def _row_rms_norm(p, gain):
    return p * lax.rsqrt(jnp.mean(p * p, axis=-1, keepdims=True) + EPS) * gain


def _inproj_kernel(x_ref, mod_ref, w_ref, wkc_ref, wvc_ref, wqb_ref, ones_ref,
                   gk_ref, gq_ref, gkv_ref, gcq_ref,
                   cosa_ref, sna_ref, spa_ref, cosc_ref, snc_ref, spc_ref,
                   ka_ref, va_ref, kc_ref, vc_ref, u_ref, qa_ref, qc_ref, gate_ref, *, rope):
    mod = mod_ref[0]
    h = (_layer_norm(x_ref[0]) * (1.0 + mod[1:2]) + mod[0:1]).astype(BF16)

    def proj(a, b):
        return _dot(h, w_ref[:, a:b])

    def rope_a(t):
        return _rope(t, cosa_ref[...], sna_ref[...], spa_ref[...], A_HEAD_DIM // 4) if rope else t

    def rope_c(t):
        return _rope(t, cosc_ref[...], snc_ref[...], spc_ref[...], C_ROPE // 4) if rope else t

    kt = rope_a(_head_rms_norm(proj(P_AK, P_AV), ones_ref, gk_ref[...])).T.astype(BF16)
    v = proj(P_AV, P_CKV).astype(BF16)
    for hh in range(A_KV_HEADS):
        ka_ref[0, hh] = kt[hh * A_HEAD_DIM:(hh + 1) * A_HEAD_DIM, :]
        va_ref[0, hh] = v[:, hh * A_HEAD_DIM:(hh + 1) * A_HEAD_DIM]

    ckv = _row_rms_norm(proj(P_CKV, P_CKR), gkv_ref[...]).astype(BF16)
    k_rope = rope_c(proj(P_CKR, P_U))
    k_nope = _dot(ckv, wkc_ref[...])
    vc = _dot(ckv, wvc_ref[...]).astype(BF16)
    for hh in range(C_HEADS):
        kc_ref[0, hh] = (k_nope[:, hh * C_HEAD_PAD:(hh + 1) * C_HEAD_PAD] + k_rope).T.astype(BF16)
        vc_ref[0, hh] = vc[:, hh * C_VDIM:(hh + 1) * C_VDIM]

    u_ref[0] = proj(P_U, P_AQ)

    for t in range(A_WIDTH // LANE):
        q = _head_rms_norm(proj(P_AQ + t * LANE, P_AQ + (t + 1) * LANE), ones_ref, gq_ref[...])
        qa_ref[0, :, t * LANE:(t + 1) * LANE] = rope_a(q).astype(BF16)

    cq = _row_rms_norm(proj(P_CQ, P_GATE), gcq_ref[...]).astype(BF16)
    for hh in range(C_HEADS):
        q = rope_c(_dot(cq, wqb_ref[:, hh * C_HEAD_PAD:(hh + 1) * C_HEAD_PAD]))
        qc_ref[0, :, hh * C_HEAD_PAD:(hh + 1) * C_HEAD_PAD] = (q * (C_QK_DIM ** -0.5)).astype(BF16)

    for t in range(N_BRANCH):
        a = P_GATE + t * D_MODEL
        gate_ref[0, :, t * D_MODEL:(t + 1) * D_MODEL] = jax.nn.sigmoid(proj(a, a + D_MODEL))


def _inproj(x, mod, lw, tabs, rope):
    bsz, n_tok, _ = x.shape
    nb = n_tok // ROW_BLOCK
    per_batch_mod = mod.shape[0] > 1

    def row_spec(width):
        return pl.BlockSpec((1, ROW_BLOCK, width), lambda b, i: (b, i, 0))

    def head_spec(heads, width):
        return pl.BlockSpec((1, heads, ROW_BLOCK, width), lambda b, i: (b, 0, i, 0))

    tab_spec = pl.BlockSpec((ROW_BLOCK, LANE), lambda b, i: (i, 0))
    mod_spec = pl.BlockSpec((1, N_MOD, D_MODEL), (lambda b, i: (b, 0, 0)) if per_batch_mod else (lambda b, i: (0, 0, 0)))
    consts = [lw['w_in'], lw['w_kc'], lw['w_vc'], lw['w_qb'], lw['ones'],
              lw['gk'], lw['gq'], lw['gkv'], lw['gcq']]
    def head_t_spec(heads, width):
        return pl.BlockSpec((1, heads, width, ROW_BLOCK), lambda b, i: (b, 0, 0, i))

    out_shape = [
        jax.ShapeDtypeStruct((bsz, A_KV_HEADS, A_HEAD_DIM, n_tok), BF16),
        jax.ShapeDtypeStruct((bsz, A_KV_HEADS, n_tok, A_HEAD_DIM), BF16),
        jax.ShapeDtypeStruct((bsz, C_HEADS, C_HEAD_PAD, n_tok), BF16),
        jax.ShapeDtypeStruct((bsz, C_HEADS, n_tok, C_VDIM), BF16),
        jax.ShapeDtypeStruct((bsz, n_tok, S5_CH), F32),
        jax.ShapeDtypeStruct((bsz, n_tok, A_WIDTH), BF16),
        jax.ShapeDtypeStruct((bsz, n_tok, C_HEADS * C_HEAD_PAD), BF16),
        jax.ShapeDtypeStruct((bsz, n_tok, N_BRANCH * D_MODEL), F32),
    ]
    out_specs = [head_t_spec(A_KV_HEADS, A_HEAD_DIM), head_spec(A_KV_HEADS, A_HEAD_DIM),
                 head_t_spec(C_HEADS, C_HEAD_PAD), head_spec(C_HEADS, C_VDIM),
                 row_spec(S5_CH), row_spec(A_WIDTH), row_spec(C_HEADS * C_HEAD_PAD),
                 row_spec(N_BRANCH * D_MODEL)]
    return pl.pallas_call(
        functools.partial(_inproj_kernel, rope=rope),
        grid=(bsz, nb),
        in_specs=[row_spec(D_MODEL), mod_spec] + [_const_spec(a.shape) for a in consts] + [tab_spec] * 6,
        out_specs=out_specs,
        out_shape=out_shape,
        compiler_params=_params(2),
        name="inproj",
    )(x, mod, *consts, *tabs)


def _scores(q_ref, q_rows, kt_refs, s_ref, m_ref, *, shared_kv, dk):
    tq = ATTN_TQ
    q = q_ref[0, q_rows, :]
    if shared_kv:
        lhs = [(slice(0, 2 * tq), jnp.concatenate([q[:, :dk], q[:, dk:]], axis=0), 0)]
    else:
        lhs = [(slice(hh * tq, (hh + 1) * tq), q[:, hh * dk:(hh + 1) * dk], hh) for hh in range(2)]
    for rows, qh, hh in lhs:
        m = None
        off = 0
        for kt_ref in kt_refs:
            n = kt_ref.shape[-1]
            for c0 in range(0, n, ATTN_KEY_CHUNK):
                c1 = min(c0 + ATTN_KEY_CHUNK, n)
                s = _dot(qh, kt_ref[0, hh, :, c0:c1])
                s_ref[rows, off + c0:off + c1] = s
                mc = s.max(axis=-1, keepdims=True)
                m = mc if m is None else jnp.maximum(m, mc)
            off += n
        m_ref[rows] = m


def _exp_scores(s_ref, m_ref, p_ref, l_ref):
    m = m_ref[...]
    n = s_ref.shape[-1]
    l = None
    for c0 in range(0, n, ATTN_KEY_CHUNK):
        c1 = min(c0 + ATTN_KEY_CHUNK, n)
        p = jnp.exp(s_ref[:, c0:c1] - m)
        lc = p.sum(axis=-1, keepdims=True)
        l = lc if l is None else l + lc
        p_ref[:, c0:c1] = p.astype(BF16)
    l_ref[...] = l


def _weighted_values(p_ref, l_ref, v_refs, o_ref, o_rows, *, shared_kv, dv):
    tq = ATTN_TQ
    for hh in range(2):
        rows = slice(hh * tq, (hh + 1) * tq)
        o = None
        off = 0
        for v_ref in v_refs:
            n = v_ref.shape[2]
            part = _dot(p_ref[rows, off:off + n], v_ref[0, 0 if shared_kv else hh])
            o = part if o is None else o + part
            off += n
        o_ref[0, o_rows, hh * dv:(hh + 1) * dv] = (o / l_ref[rows]).astype(BF16)


def _attn_pipe_kernel(q_ref, kt_lat, kt_ctx, v_lat, v_ctx, o_ref,
                      s_a, s_b, p_a, p_b, m_a, m_b, l_a, l_b, *, shared_kv, dk, dv):
    g = pl.program_id(0)

    @pl.when(g == 0)
    def _():
        s_b[...] = jnp.zeros_like(s_b)
        m_b[...] = jnp.zeros_like(m_b)
        p_a[...] = jnp.zeros_like(p_a)
        l_a[...] = jnp.ones_like(l_a)

    kts, vs = [kt_lat, kt_ctx], [v_lat, v_ctx]
    rows = slice(0, ATTN_TQ)

    def tick(s_new, m_new, s_old, m_old, p_new, l_new, p_old, l_old):
        _scores(q_ref, rows, kts, s_new, m_new, shared_kv=shared_kv, dk=dk)
        _exp_scores(s_old, m_old, p_new, l_new)
        _weighted_values(p_old, l_old, vs, o_ref, rows, shared_kv=shared_kv, dv=dv)

    @pl.when(g % 2 == 0)
    def _():
        tick(s_a, m_a, s_b, m_b, p_b, l_b, p_a, l_a)

    @pl.when(g % 2 == 1)
    def _():
        tick(s_b, m_b, s_a, m_a, p_a, l_a, p_b, l_b)


def _attn_ctx_kernel(q_ref, kt_ref, v_ref, o_ref, *, shared_kv, dk, dv):
    for hh in range(2):
        kv = 0 if shared_kv else hh
        s = _dot(q_ref[0, :, hh * dk:(hh + 1) * dk], kt_ref[0, kv])
        p = jnp.exp(s - s.max(axis=-1, keepdims=True))
        o = _dot(p.astype(BF16), v_ref[0, kv]) / p.sum(axis=-1, keepdims=True)
        o_ref[0, :, hh * dv:(hh + 1) * dv] = o.astype(BF16)


def _attention(q, kt_lat, kt_ctx, v_lat, v_ctx, *, heads, kv_heads, dk, dv, name):
    bsz, lq, _ = q.shape
    shared_kv = kv_heads < heads
    group = heads // kv_heads
    n_pairs = heads // 2
    n_qb = lq // ATTN_TQ
    n_steps = bsz * n_pairs * n_qb
    lag = 2
    kv_blk = 1 if shared_kv else 2
    n_lat, n_ctx = kt_lat.shape[-1], kt_ctx.shape[-1]

    def split(g):
        return g // (n_pairs * n_qb), (g // n_qb) % n_pairs, g % n_qb

    def q_map(g):
        b, p, i = split(jnp.minimum(g, n_steps - 1))
        return b, i, p

    def k_map(g):
        b, p, _ = split(jnp.minimum(g, n_steps - 1))
        return b, (2 * p) // group if shared_kv else p, 0, 0

    def v_map(g):
        b, p, _ = split(jnp.maximum(g - lag, 0))
        return b, (2 * p) // group if shared_kv else p, 0, 0

    def o_map(g):
        b, p, i = split(jnp.maximum(g - lag, 0))
        return b, i, p

    rows, keys = 2 * ATTN_TQ, n_lat + n_ctx
    return pl.pallas_call(
        functools.partial(_attn_pipe_kernel, shared_kv=shared_kv, dk=dk, dv=dv),
        grid=(n_steps + lag,),
        in_specs=[pl.BlockSpec((1, ATTN_TQ, 2 * dk), q_map),
                  pl.BlockSpec((1, kv_blk, dk, n_lat), k_map), pl.BlockSpec((1, kv_blk, dk, n_ctx), k_map),
                  pl.BlockSpec((1, kv_blk, n_lat, dv), v_map), pl.BlockSpec((1, kv_blk, n_ctx, dv), v_map)],
        out_specs=pl.BlockSpec((1, ATTN_TQ, 2 * dv), o_map),
        out_shape=jax.ShapeDtypeStruct((bsz, lq, heads * dv), BF16),
        scratch_shapes=[pltpu.VMEM((rows, keys), F32), pltpu.VMEM((rows, keys), F32),
                        pltpu.VMEM((rows, keys), BF16), pltpu.VMEM((rows, keys), BF16)]
                       + [pltpu.VMEM((rows, 1), F32)] * 4,
        compiler_params=_params(1),
        name=name,
    )(q, kt_lat, kt_ctx, v_lat, v_ctx)


def _attention_ctx(q, kt, v, *, heads, kv_heads, dk, dv, name):
    bsz, lq, _ = q.shape
    shared_kv = kv_heads < heads
    group = heads // kv_heads
    kv_blk = 1 if shared_kv else 2
    kmap = (lambda b, p: (b, (2 * p) // group, 0, 0)) if shared_kv else (lambda b, p: (b, p, 0, 0))
    return pl.pallas_call(
        functools.partial(_attn_ctx_kernel, shared_kv=shared_kv, dk=dk, dv=dv),
        grid=(bsz, heads // 2),
        in_specs=[pl.BlockSpec((1, lq, 2 * dk), lambda b, p: (b, 0, p)),
                  pl.BlockSpec((1, kv_blk, dk, kt.shape[-1]), kmap),
                  pl.BlockSpec((1, kv_blk, v.shape[2], dv), kmap)],
        out_specs=pl.BlockSpec((1, lq, 2 * dv), lambda b, p: (b, 0, p)),
        out_shape=jax.ShapeDtypeStruct((bsz, lq, heads * dv), BF16),
        compiler_params=_params(2),
        name=name,
    )(q, kt, v)


def _s5_kernel(u_ref, cin_ref, wb_ref, a_ref, apow_ref, wc_ref, yprev_ref, d_ref, wglu_ref,
               out_ref, cout_ref, bu_r, bu_i, xb, car, *, reverse, glu):
    @pl.when(pl.program_id(1) == 0)
    def _():
        car[...] = cin_ref[0]

    ub = u_ref[0].astype(BF16)
    n_tiles = S5_CH // LANE
    tile_states = S5_LANES // n_tiles
    for t in range(n_tiles):
        r = _dot(ub[:, t * LANE:(t + 1) * LANE], wb_ref[t])
        bu_r[:, t * tile_states:(t + 1) * tile_states] = r[:, :tile_states]
        bu_i[:, t * tile_states:(t + 1) * tile_states] = r[:, tile_states:]

    steps = range(S5_SUB_LEN - 1, -1, -1) if reverse else range(S5_SUB_LEN)
    subs = range(S5_SUB - 1, -1, -1) if reverse else range(S5_SUB)
    pack_rows = 2 * S5_SUB
    for c in range(S5_LANES // S5_LANE_CHUNK):
        ls = slice(c * S5_LANE_CHUNK, (c + 1) * S5_LANE_CHUNK)
        ar = jnp.broadcast_to(a_ref[0:1, ls], (S5_SUB, S5_LANE_CHUNK))
        ai = jnp.broadcast_to(a_ref[1:2, ls], (S5_SUB, S5_LANE_CHUNK))
        xr = jnp.zeros((S5_SUB, S5_LANE_CHUNK), F32)
        xi = jnp.zeros((S5_SUB, S5_LANE_CHUNK), F32)
        for i in steps:
            rows = slice(i * S5_SUB, (i + 1) * S5_SUB)
            xr, xi = (ar * xr - ai * xi + bu_r[rows, ls], ar * xi + ai * xr + bu_i[rows, ls])
            bu_r[rows, ls] = xr
            bu_i[rows, ls] = xi
        cr, ci = car[0:1, ls], car[1:2, ls]
        a_sub_r, a_sub_i = a_ref[2:3, ls], a_ref[3:4, ls]
        crs, cis = [None] * S5_SUB, [None] * S5_SUB
        for j in subs:
            crs[j], cis[j] = cr, ci
            cr, ci = (a_sub_r * cr - a_sub_i * ci + xr[j:j + 1], a_sub_r * ci + a_sub_i * cr + xi[j:j + 1])
        car[0:1, ls] = cr
        car[1:2, ls] = ci
        cmr = jnp.concatenate(crs * (pack_rows // S5_SUB), axis=0)
        cmi = jnp.concatenate(cis * (pack_rows // S5_SUB), axis=0)
        col = (c * S5_LANE_CHUNK // tile_states) * 2 * tile_states + (c * S5_LANE_CHUNK) % tile_states
        for g in range(S5_CHUNK // pack_rows):
            rows = slice(g * pack_rows, (g + 1) * pack_rows)
            pr, pi = apow_ref[0, rows, ls], apow_ref[1, rows, ls]
            xb[rows, col:col + S5_LANE_CHUNK] = (bu_r[rows, ls] + (pr * cmr - pi * cmi)).astype(BF16)
            xb[rows, col + tile_states:col + tile_states + S5_LANE_CHUNK] = (
                bu_i[rows, ls] + (pr * cmi + pi * cmr)).astype(BF16)
    cout_ref[0] = car[...]

    ys = [_dot(xb[:, t * 2 * tile_states:(t + 1) * 2 * tile_states], wc_ref[t]) for t in range(n_tiles)]
    y = jnp.concatenate(ys, axis=-1)
    if glu:
        y = y + yprev_ref[0] + d_ref[...] * u_ref[0]
        hg = _dot(jax.nn.gelu(y).astype(BF16), wglu_ref[...])
        out_ref[0] = (hg[:, :S5_CH] * jax.nn.sigmoid(hg[:, S5_CH:])).astype(out_ref.dtype)
    else:
        out_ref[0] = y


def _s5_scan(u, carry_in, sw, yprev, d, w_glu, *, reverse, glu):
    bsz, n_tok, _ = u.shape
    nc = n_tok // S5_CHUNK
    order = (lambda b, i: (b, nc - 1 - i, 0)) if reverse else (lambda b, i: (b, i, 0))
    tile_states = S5_LANES // (S5_CH // LANE)
    assert S5_LANE_CHUNK == tile_states
    row_spec = pl.BlockSpec((1, S5_CHUNK, S5_CH), order)
    carry_spec = pl.BlockSpec((1, 2, S5_LANES), lambda b, i: (b, 0, 0))
    consts = [sw['wb'], sw['a'], sw['apow'], sw['wc']]
    out, carry = pl.pallas_call(
        functools.partial(_s5_kernel, reverse=reverse, glu=glu),
        grid=(bsz, nc),
        in_specs=[row_spec, carry_spec] + [_const_spec(a.shape) for a in consts]
                 + [row_spec, _const_spec(d.shape), _const_spec(w_glu.shape)],
        out_specs=[row_spec, carry_spec],
        out_shape=[jax.ShapeDtypeStruct((bsz, n_tok, S5_CH), BF16 if glu else F32),
                   jax.ShapeDtypeStruct((bsz, 2, S5_LANES), F32)],
        scratch_shapes=[pltpu.VMEM((S5_CHUNK, S5_LANES), F32), pltpu.VMEM((S5_CHUNK, S5_LANES), F32),
                        pltpu.VMEM((S5_CHUNK, 2 * S5_LANES), BF16), pltpu.VMEM((2, S5_LANES), F32)],
        compiler_params=_params(2),
        name="s5_bwd" if reverse else "s5_fwd",
    )(u, carry_in, *consts, yprev, d, w_glu)
    return out, carry


def _merge_kernel(x_ref, mod_ref, ya_ref, ys_ref, yc_ref, gate_ref, wa_ref, ws_ref, wc_ref, wo_ref,
                  g_ref, b_ref, o_ref):
    gate = gate_ref[0]
    merged = (gate[:, :D_MODEL] * _dot(ya_ref[0], wa_ref[...])
              + gate[:, D_MODEL:2 * D_MODEL] * _dot(ys_ref[0], ws_ref[...])
              + gate[:, 2 * D_MODEL:] * _dot(yc_ref[0], wc_ref[...]))
    mix = _dot(merged.astype(BF16), wo_ref[...])
    y = DEEPNORM_ALPHA * x_ref[0] + mod_ref[0][2:3] * mix
    o_ref[0] = _layer_norm(y) * g_ref[...] + b_ref[...]


def _mlp_kernel(x_ref, mod_ref, wu_ref, wd_ref, g_ref, b_ref, o_ref):
    x = x_ref[0]
    mod = mod_ref[0]
    h = (_layer_norm(x) * (1.0 + mod[4:5]) + mod[3:4]).astype(BF16)
    up = jnp.square(jnp.maximum(_dot(h, wu_ref[...]), 0.0)).astype(BF16)
    y = DEEPNORM_ALPHA * x + mod[5:6] * _dot(up, wd_ref[...])
    o_ref[0] = _layer_norm(y) * g_ref[...] + b_ref[...]


def _token_call(kernel, name, x, mod, rows, consts):
    bsz, n_tok, _ = x.shape
    per_batch_mod = mod.shape[0] > 1

    def row_spec(width):
        return pl.BlockSpec((1, ROW_BLOCK, width), lambda b, i: (b, i, 0))

    mod_spec = pl.BlockSpec((1, N_MOD, D_MODEL), (lambda b, i: (b, 0, 0)) if per_batch_mod else (lambda b, i: (0, 0, 0)))
    return pl.pallas_call(
        kernel,
        grid=(bsz, n_tok // ROW_BLOCK),
        in_specs=[row_spec(D_MODEL), mod_spec] + [row_spec(r.shape[-1]) for r in rows]
                 + [_const_spec(a.shape) for a in consts],
        out_specs=row_spec(D_MODEL),
        out_shape=jax.ShapeDtypeStruct((bsz, n_tok, D_MODEL), F32),
        compiler_params=_params(2),
        name=name,
    )(x, mod, *rows, *consts)


def _axial_rope_tables(rows, dim):
    half = dim // 2
    inv = ROPE_THETA ** (-jnp.arange(0, half, 2, dtype=F32) / half)
    row = jnp.repeat(jnp.arange(rows, dtype=F32), GRID_W)
    col = jnp.tile(jnp.arange(GRID_W, dtype=F32), rows)
    ang_r = row[:, None] * inv
    ang_c = col[:, None] * inv
    ang = jnp.concatenate([ang_r, ang_r, ang_c, ang_c], axis=-1)
    return jnp.cos(ang), jnp.sin(ang)


def _rope_lane_tables(n_tok, dim, lane_off):
    cos, sin = _axial_rope_tables(n_tok // GRID_W, dim)
    reps = (LANE - lane_off) // dim if lane_off == 0 else 1
    cos_t = jnp.ones((n_tok, LANE), F32).at[:, lane_off:lane_off + reps * dim].set(jnp.tile(cos, (1, reps)))
    sin_t = jnp.zeros((n_tok, LANE), F32).at[:, lane_off:lane_off + reps * dim].set(jnp.tile(sin, (1, reps)))
    quarter = dim // 4
    first = (jnp.arange(LANE) % (2 * quarter)) < quarter
    return cos_t, jnp.where(first, -sin_t, 0.0), jnp.where(first, 0.0, sin_t)


def _block_diag(blocks):
    n, r, c = blocks.shape
    eye = jnp.eye(n, dtype=blocks.dtype)
    return (eye[:, None, :, None] * blocks[:, :, None, :]).reshape(n * r, n * c)


def _s5_discretize(a_re, a_im, log_dt, b_re, b_im):
    dt = jnp.exp(log_dt)[:, None]
    mag = jnp.exp(a_re * dt)
    abar_r = mag * jnp.cos(a_im * dt)
    abar_i = mag * jnp.sin(a_im * dt)
    den = a_re * a_re + a_im * a_im
    nr = abar_r - 1.0
    coef_r = (nr * a_re + abar_i * a_im) / den
    coef_i = (abar_i * a_re - nr * a_im) / den
    bbar_r = coef_r[..., None] * b_re - coef_i[..., None] * b_im
    bbar_i = coef_r[..., None] * b_im + coef_i[..., None] * b_re
    return abar_r, abar_i, bbar_r, bbar_i


def _s5_weights(a_re, a_im, log_dt, b_re, b_im, c_re, c_im, reverse):
    abar_r, abar_i, bbar_r, bbar_i = _s5_discretize(a_re, a_im, log_dt, b_re, b_im)
    groups_per_tile = LANE // S5_GROUP_CH
    n_tiles = S5_GROUPS // groups_per_tile

    def in_tile(bbar):
        blk = jnp.swapaxes(bbar, 1, 2).reshape(n_tiles, groups_per_tile, S5_GROUP_CH, S5_STATE)
        return jax.vmap(_block_diag)(blk)

    def out_tile(cm):
        blk = jnp.swapaxes(cm, 1, 2).reshape(n_tiles, groups_per_tile, S5_STATE, S5_GROUP_CH)
        return jax.vmap(_block_diag)(blk)

    wb = jnp.concatenate([in_tile(bbar_r), in_tile(bbar_i)], axis=-1).astype(BF16)
    wc = jnp.concatenate([out_tile(c_re), -out_tile(c_im)], axis=1).astype(BF16)

    def step(carry, _):
        pr, pi = carry
        nxt = (pr * abar_r - pi * abar_i, pr * abar_i + pi * abar_r)
        return nxt, nxt
    _, (pows_r, pows_i) = lax.scan(step, (jnp.ones_like(abar_r), jnp.zeros_like(abar_r)), None, length=S5_SUB_LEN)
    pows_r = pows_r.reshape(S5_SUB_LEN, S5_LANES)
    pows_i = pows_i.reshape(S5_SUB_LEN, S5_LANES)
    if reverse:
        pows_r, pows_i = pows_r[::-1], pows_i[::-1]
    sub_r = pows_r[0] if reverse else pows_r[-1]
    sub_i = pows_i[0] if reverse else pows_i[-1]
    a = jnp.stack([abar_r.reshape(-1), abar_i.reshape(-1), sub_r, sub_i])
    apow = jnp.repeat(jnp.stack([pows_r, pows_i]), S5_SUB, axis=1)
    return {'wb': wb, 'a': a, 'apow': apow, 'wc': wc}


def _interleave(t):
    b, n, w = t.shape
    return t.reshape(b, n // S5_CHUNK, S5_SUB, S5_SUB_LEN, w).swapaxes(2, 3).reshape(b, n, w)


def _deinterleave(t):
    b, n, w = t.shape
    return t.reshape(b, n // S5_CHUNK, S5_SUB_LEN, S5_SUB, w).swapaxes(2, 3).reshape(b, n, w)


def _layer_weights(l, w_in, a_q_gain, a_k_gain, c_q_a_gain, c_kv_a_gain, c_w_qb, c_w_kvb):
    w = w_in[l]
    ckr = jnp.zeros((D_MODEL, LANE), F32).at[:, C_NOPE:C_QK_DIM].set(w[:, OFF_CKR:OFF_U])
    w_re = jnp.concatenate([w[:, OFF_AK:OFF_CKR], ckr, w[:, OFF_U:]], axis=1).astype(BF16)
    assert w_re.shape[1] == P_COLS
    qb = c_w_qb[l].reshape(C_Q_RANK, C_HEADS, C_QK_DIM)
    qb = jnp.pad(qb, ((0, 0), (0, 0), (0, C_HEAD_PAD - C_QK_DIM))).reshape(C_Q_RANK, C_HEADS * C_HEAD_PAD)
    kvb = c_w_kvb[l].reshape(C_KV_RANK, C_HEADS, C_NOPE + C_VDIM)
    w_kc = jnp.pad(kvb[:, :, :C_NOPE], ((0, 0), (0, 0), (0, C_HEAD_PAD - C_NOPE)))
    w_kc = w_kc.reshape(C_KV_RANK, C_HEADS * C_HEAD_PAD)
    w_vc = kvb[:, :, C_NOPE:].reshape(C_KV_RANK, C_HEADS * C_VDIM)
    heads_per_tile = LANE // A_HEAD_DIM
    ones = _block_diag(jnp.full((heads_per_tile, A_HEAD_DIM, A_HEAD_DIM), 1.0 / A_HEAD_DIM, F32))
    return {
        'w_in': w_re, 'w_kc': w_kc.astype(BF16), 'w_vc': w_vc.astype(BF16), 'w_qb': qb.astype(BF16),
        'ones': ones.astype(BF16),
        'gk': jnp.tile(a_k_gain[l], heads_per_tile)[None],
        'gq': jnp.tile(a_q_gain[l], heads_per_tile)[None] * (A_HEAD_DIM ** -0.5),
        'gkv': c_kv_a_gain[l][None], 'gcq': c_q_a_gain[l][None],
    }


def kernel(x, c, ctx, c_ctx, w_mod, b_mod, w_in, a_q_gain, a_k_gain, c_q_a_gain, c_kv_a_gain, c_w_qb, c_w_kvb, s5_a_re, s5_a_im, s5_log_dt, s5_b_re, s5_b_im, s5_c_re, s5_c_im, s5_d, s5_w_glu, w_branch_a, w_branch_s5, w_branch_c, w_out, ln1_g, ln1_b, w_up, w_down, ln2_g, ln2_b):
    bsz, seq, _ = x.shape
    n_ctx = ctx.shape[1]
    tabs_lat = _rope_lane_tables(seq, A_HEAD_DIM, 0) + _rope_lane_tables(seq, C_ROPE, C_NOPE)
    c_all = jnp.zeros((8, D_MODEL), F32).at[:bsz].set(c).at[bsz].set(c_ctx)
    zero_carry = jnp.zeros((bsz, 2, S5_LANES), F32)

    for l in range(DEPTH):
        last = l == DEPTH - 1
        lw = _layer_weights(l, w_in, a_q_gain, a_k_gain, c_q_a_gain, c_kv_a_gain, c_w_qb, c_w_kvb)
        sw = [_s5_weights(s5_a_re[l, dr], s5_a_im[l, dr], s5_log_dt[l, dr], s5_b_re[l, dr], s5_b_im[l, dr],
                          s5_c_re[l, dr], s5_c_im[l, dr], dr == 1) for dr in range(2)]
        d_row = s5_d[l][None]
        w_glu = s5_w_glu[l].astype(BF16)
        merge_w = [w_branch_a[l].astype(BF16), w_branch_s5[l].astype(BF16), w_branch_c[l].astype(BF16),
                   w_out[l].astype(BF16), ln1_g[l][None], ln1_b[l][None]]
        mlp_w = [w_up[l].astype(BF16), w_down[l].astype(BF16), ln2_g[l][None], ln2_b[l][None]]

        mod = _modulation(c_all, w_mod, b_mod, l).reshape(8, N_MOD, D_MODEL)
        mod_lat, mod_ctx = mod[:bsz], mod[bsz:bsz + 1]

        ka_c, va_c, kc_c, vc_c, u_c, qa_c, qc_c, gate_c = _inproj(ctx, mod_ctx, lw, tabs_lat, rope=False)
        ka, va, kc, vc, u, qa, qc, gate = _inproj(x, mod_lat, lw, tabs_lat, rope=True)

        u_c, u = _interleave(u_c), _interleave(u)
        yf_c, carry_f = _s5_scan(u_c, zero_carry, sw[0], u_c, d_row, w_glu, reverse=False, glu=False)
        yf, _ = _s5_scan(u, carry_f, sw[0], u, d_row, w_glu, reverse=False, glu=False)
        ys_c, carry_b = _s5_scan(u_c, zero_carry, sw[1], yf_c, d_row, w_glu, reverse=True, glu=True)
        ys, _ = _s5_scan(u, carry_b, sw[1], yf, d_row, w_glu, reverse=True, glu=True)
        ys_c, ys = _deinterleave(ys_c), _deinterleave(ys)

        ya = _attention(qa, ka, ka_c, va, va_c, heads=A_HEADS, kv_heads=A_KV_HEADS,
                        dk=A_HEAD_DIM, dv=A_HEAD_DIM, name="attn_a")
        yc = _attention(qc, kc, kc_c, vc, vc_c, heads=C_HEADS, kv_heads=C_HEADS,
                        dk=C_HEAD_PAD, dv=C_VDIM, name="attn_c")
        x_mid = _token_call(_merge_kernel, "merge", x, mod_lat, [ya, ys, yc, gate], merge_w)
        x_next = _token_call(_mlp_kernel, "mlp", x_mid, mod_lat, [], mlp_w)

        if not last:
            ya_c = _attention_ctx(qa_c, ka_c, va_c, heads=A_HEADS, kv_heads=A_KV_HEADS,
                                  dk=A_HEAD_DIM, dv=A_HEAD_DIM, name="attn_a_ctx")
            yc_c = _attention_ctx(qc_c, kc_c, vc_c, heads=C_HEADS, kv_heads=C_HEADS,
                                  dk=C_HEAD_PAD, dv=C_VDIM, name="attn_c_ctx")
            ctx_mid = _token_call(_merge_kernel, "merge_ctx", ctx, mod_ctx, [ya_c, ys_c, yc_c, gate_c], merge_w)
            ctx = _token_call(_mlp_kernel, "mlp_ctx", ctx_mid, mod_ctx, [], mlp_w)
        x = x_next
    return x
```

```python
import functools
import math

import jax
import jax.numpy as jnp
from jax import lax
from jax.experimental import pallas as pl
from jax.experimental.pallas import tpu as pltpu

F32 = jnp.float32
BF16 = jnp.bfloat16

D_MODEL = 1024
DEPTH = 2
GRID_W = 64
ROPE_THETA = 10000.0
EPS = 1e-6

A_HEADS = 8
A_KV_HEADS = 2
A_HEAD_DIM = 64
A_WIDTH = A_HEADS * A_HEAD_DIM

S5_CH = 512
S5_GROUP_CH = 16
S5_GROUPS = S5_CH // S5_GROUP_CH
S5_STATE = 64
S5_LANES = S5_GROUPS * S5_STATE

C_HEADS = 8
C_NOPE = 64
C_ROPE = 32
C_VDIM = 64
C_Q_RANK = 768
C_KV_RANK = 256
C_QK_DIM = C_NOPE + C_ROPE
C_WIDTH = C_HEADS * C_VDIM

D_FF = 4 * D_MODEL
N_BRANCH = 3
N_MOD = 6
DEEPNORM_ALPHA = (2.0 * DEPTH) ** 0.25
LOG2_E = math.log2(math.e)

OFF_AK = 0
OFF_AV = OFF_AK + A_KV_HEADS * A_HEAD_DIM
OFF_CKV = OFF_AV + A_KV_HEADS * A_HEAD_DIM
OFF_CKR = OFF_CKV + C_KV_RANK
OFF_U = OFF_CKR + C_ROPE
OFF_AQ = OFF_U + S5_CH
OFF_CQ = OFF_AQ + A_WIDTH
OFF_GATE = OFF_CQ + C_Q_RANK
N_IN_COLS = OFF_GATE + N_BRANCH * D_MODEL

LANE = 128
C_HEAD_PAD = LANE

P_AK = 0
P_AV = P_AK + LANE
P_CKV = P_AV + LANE
P_CKR = P_CKV + C_KV_RANK
P_U = P_CKR + LANE
P_AQ = P_U + S5_CH
P_CQ = P_AQ + A_WIDTH
P_GATE = P_CQ + C_Q_RANK
P_COLS = P_GATE + N_BRANCH * D_MODEL

ROW_BLOCK = 512
ATTN_TQ = 256
ATTN_KEY_CHUNK = 1024
S5_CHUNK = 256
S5_SUB = 8
S5_SUB_LEN = S5_CHUNK // S5_SUB
S5_LANE_CHUNK = 512
VMEM_LIMIT = 56 * 1024 * 1024


def _const_spec(shape):
    nd = len(shape)
    return pl.BlockSpec(shape, lambda *_: (0,) * nd, pipeline_mode=pl.Buffered(1))


def _params(n_grid):
    return pltpu.CompilerParams(dimension_semantics=("arbitrary",) * n_grid,
                                vmem_limit_bytes=VMEM_LIMIT)


def _layer_norm(x):
    mu = jnp.mean(x, axis=-1, keepdims=True)
    xc = x - mu
    var = jnp.mean(xc * xc, axis=-1, keepdims=True)
    return xc * lax.rsqrt(var + EPS)


def _dot(a, b):
    return jnp.dot(a, b, preferred_element_type=F32)


def _mod_kernel(c_ref, w_ref, b_ref, o_ref):
    s = jax.nn.silu(c_ref[...]).astype(BF16)
    o_ref[...] = _dot(s, w_ref[...].astype(BF16)) + b_ref[...]


def _modulation(c_all, w_mod, b_mod, layer):
    n = N_MOD * D_MODEL
    bn = n // 4
    return pl.pallas_call(
        _mod_kernel,
        grid=(n // bn,),
        in_specs=[pl.BlockSpec((8, D_MODEL), lambda j: (0, 0)),
                  pl.BlockSpec((None, D_MODEL, bn), lambda j: (layer, 0, j)),
                  pl.BlockSpec((None, 1, bn), lambda j: (layer, 0, j))],
        out_specs=pl.BlockSpec((8, bn), lambda j: (0, j)),
        out_shape=jax.ShapeDtypeStruct((8, n), F32),
        compiler_params=_params(1),
        name="mod",
    )(c_all, w_mod, b_mod.reshape(DEPTH, 1, n))


def _rope(x, cos, sin_next, sin_prev, shift):
    n = x.shape[-1]
    return x * cos + pltpu.roll(x, n - shift, 1) * sin_next + pltpu.roll(x, shift, 1) * sin_prev


def _head_rms_norm(p, ones_ref, gain):
    sq = p * p
    hi = sq.astype(BF16)
    lo = (sq - hi.astype(F32)).astype(BF16)
    ms = _dot(jnp.concatenate([hi, lo], axis=1), ones_ref[...])
    return p * lax.rsqrt(ms + EPS) * gain


def _row_rms_norm(p, gain):
    return p * lax.rsqrt(jnp.mean(p * p, axis=-1, keepdims=True) + EPS) * gain


def _value_tile(v, head):
    half = LANE // 2
    t = v[:, (head // 2) * LANE:(head // 2 + 1) * LANE]
    if head % 2:
        t = pltpu.roll(t, half, 1)
    low = lax.broadcasted_iota(jnp.int32, t.shape, 1) < half
    return jnp.where(low, t, 1.0).astype(BF16)


def _inproj_kernel(x_ref, mod_ref, w_ref, wkc_ref, wvc_ref, wqb_ref, ones_ref,
                   gk_ref, gq_ref, gkv_ref, gcq_ref,
                   cosa_ref, sna_ref, spa_ref, cosc_ref, snc_ref, spc_ref,
                   ka_ref, va_ref, kc_ref, vc_ref, u_ref, qa_ref, qc_ref, gate_ref, *, rope):
    mod = mod_ref[0]
    h = (_layer_norm(x_ref[0]) * (1.0 + mod[1:2]) + mod[0:1]).astype(BF16)

    def proj(a, b):
        return _dot(h, w_ref[:, a:b])

    def rope_a(t):
        return _rope(t, cosa_ref[...], sna_ref[...], spa_ref[...], A_HEAD_DIM // 4) if rope else t

    def rope_c(t):
        return _rope(t, cosc_ref[...], snc_ref[...], spc_ref[...], C_ROPE // 4) if rope else t

    state = proj(P_AK, P_U)
    kt = rope_a(_head_rms_norm(state[:, P_AK:P_AV], ones_ref, gk_ref[...])).T.astype(BF16)
    v = state[:, P_AV:P_CKV]
    for hh in range(A_KV_HEADS):
        ka_ref[0, hh] = kt[hh * A_HEAD_DIM:(hh + 1) * A_HEAD_DIM, :]
        va_ref[0, hh] = _value_tile(v, hh)

    ckv = _row_rms_norm(state[:, P_CKV:P_CKR], gkv_ref[...]).astype(BF16)
    k_rope = rope_c(state[:, P_CKR:P_U])
    k_nope = _dot(ckv, wkc_ref[...])
    vc = _dot(ckv, wvc_ref[...])
    for hh in range(C_HEADS):
        kc_ref[0, hh] = (k_nope[:, hh * C_HEAD_PAD:(hh + 1) * C_HEAD_PAD] + k_rope).T.astype(BF16)
        vc_ref[0, hh] = _value_tile(vc, hh)

    u_ref[0] = proj(P_U, P_AQ)

    aq = proj(P_AQ, P_CQ)
    for t in range(A_WIDTH // LANE):
        q = _head_rms_norm(aq[:, t * LANE:(t + 1) * LANE], ones_ref, gq_ref[...])
        qa_ref[0, :, t * LANE:(t + 1) * LANE] = rope_a(q).astype(BF16)

    cq = _row_rms_norm(proj(P_CQ, P_GATE), gcq_ref[...]).astype(BF16)
    qc = _dot(cq, wqb_ref[...])
    for hh in range(C_HEADS):
        q = rope_c(qc[:, hh * C_HEAD_PAD:(hh + 1) * C_HEAD_PAD])
        qc_ref[0, :, hh * C_HEAD_PAD:(hh + 1) * C_HEAD_PAD] = (q * (C_QK_DIM ** -0.5 * LOG2_E)).astype(BF16)

    for t in range(N_BRANCH):
        a = P_GATE + t * D_MODEL
        gate_ref[0, :, t * D_MODEL:(t + 1) * D_MODEL] = jax.nn.sigmoid(proj(a, a + D_MODEL))


def _inproj(x, mod, lw, tabs, rope):
    bsz, n_tok, _ = x.shape
    rb = min(ROW_BLOCK, n_tok)
    nb = n_tok // rb
    per_batch_mod = mod.shape[0] > 1

    def row_spec(width):
        return pl.BlockSpec((1, rb, width), lambda b, i: (b, i, 0))

    def head_spec(heads, width):
        return pl.BlockSpec((1, heads, rb, width), lambda b, i: (b, 0, i, 0))

    tab_spec = pl.BlockSpec((rb, LANE), lambda b, i: (i, 0))
    mod_spec = pl.BlockSpec((1, N_MOD, D_MODEL), (lambda b, i: (b, 0, 0)) if per_batch_mod else (lambda b, i: (0, 0, 0)))
    consts = [lw['w_in'], lw['w_kc'], lw['w_vc'], lw['w_qb'], lw['ones'],
              lw['gk'], lw['gq'], lw['gkv'], lw['gcq']]
    def head_t_spec(heads, width):
        return pl.BlockSpec((1, heads, width, rb), lambda b, i: (b, 0, 0, i))

    out_shape = [
        jax.ShapeDtypeStruct((bsz, A_KV_HEADS, A_HEAD_DIM, n_tok), BF16),
        jax.ShapeDtypeStruct((bsz, A_KV_HEADS, n_tok, 2 * A_HEAD_DIM), BF16),
        jax.ShapeDtypeStruct((bsz, C_HEADS, C_HEAD_PAD, n_tok), BF16),
        jax.ShapeDtypeStruct((bsz, C_HEADS, n_tok, 2 * C_VDIM), BF16),
        jax.ShapeDtypeStruct((bsz, n_tok, S5_CH), F32),
        jax.ShapeDtypeStruct((bsz, n_tok, A_WIDTH), BF16),
        jax.ShapeDtypeStruct((bsz, n_tok, C_HEADS * C_HEAD_PAD), BF16),
        jax.ShapeDtypeStruct((bsz, n_tok, N_BRANCH * D_MODEL), F32),
    ]
    out_specs = [head_t_spec(A_KV_HEADS, A_HEAD_DIM), head_spec(A_KV_HEADS, 2 * A_HEAD_DIM),
                 head_t_spec(C_HEADS, C_HEAD_PAD), head_spec(C_HEADS, 2 * C_VDIM),
                 row_spec(S5_CH), row_spec(A_WIDTH), row_spec(C_HEADS * C_HEAD_PAD),
                 row_spec(N_BRANCH * D_MODEL)]
    return pl.pallas_call(
        functools.partial(_inproj_kernel, rope=rope),
        grid=(bsz, nb),
        in_specs=[row_spec(D_MODEL), mod_spec] + [_const_spec(a.shape) for a in consts] + [tab_spec] * 6,
        out_specs=out_specs,
        out_shape=out_shape,
        compiler_params=_params(2),
        name="inproj",
    )(x, mod, *consts, *tabs)


def _scores(q_ref, q_rows, kt_refs, s_ref, m_ref, *, shared_kv, dk):
    tq = ATTN_TQ
    q = q_ref[0, q_rows, :]
    if shared_kv:
        lhs = [(slice(0, 2 * tq), jnp.concatenate([q[:, :dk], q[:, dk:]], axis=0), 0)]
    else:
        lhs = [(slice(hh * tq, (hh + 1) * tq), q[:, hh * dk:(hh + 1) * dk], hh) for hh in range(2)]
    for rows, qh, hh in lhs:
        m = None
        off = 0
        for kt_ref in kt_refs:
            n = kt_ref.shape[-1]
            for c0 in range(0, n, ATTN_KEY_CHUNK):
                c1 = min(c0 + ATTN_KEY_CHUNK, n)
                s = _dot(qh, kt_ref[0, hh, :, c0:c1])
                s_ref[rows, off + c0:off + c1] = s
                mc = s.max(axis=-1, keepdims=True)
                m = mc if m is None else jnp.maximum(m, mc)
            off += n
        m_ref[rows] = m


def _exp_scores(s_ref, m_ref, p_ref):
    m = m_ref[...]
    n = s_ref.shape[-1]
    for c0 in range(0, n, ATTN_KEY_CHUNK):
        c1 = min(c0 + ATTN_KEY_CHUNK, n)
        p_ref[:, c0:c1] = jnp.exp2(s_ref[:, c0:c1] - m).astype(BF16)


def _weighted_values(p_ref, v_refs, o_ref, o_rows, *, shared_kv, dv):
    tq = ATTN_TQ
    for hh in range(2):
        rows = slice(hh * tq, (hh + 1) * tq)
        o = None
        off = 0
        for v_ref in v_refs:
            n = v_ref.shape[2]
            part = _dot(p_ref[rows, off:off + n], v_ref[0, 0 if shared_kv else hh])
            o = part if o is None else o + part
            off += n
        o_ref[0, o_rows, hh * dv:(hh + 1) * dv] = (o[:, :dv] / o[:, dv:]).astype(BF16)


def _attn_pipe_kernel(q_ref, kt_lat, kt_ctx, v_lat, v_ctx, o_ref,
                      s_a, s_b, p_a, p_b, m_a, m_b, *, shared_kv, dk, dv):
    g = pl.program_id(0)

    @pl.when(g == 0)
    def _():
        s_b[...] = jnp.zeros_like(s_b)
        m_b[...] = jnp.zeros_like(m_b)
        p_a[...] = jnp.ones_like(p_a)

    kts, vs = [kt_lat, kt_ctx], [v_lat, v_ctx]
    rows = slice(0, ATTN_TQ)

    def tick(s_new, m_new, s_old, m_old, p_new, p_old):
        _scores(q_ref, rows, kts, s_new, m_new, shared_kv=shared_kv, dk=dk)
        _exp_scores(s_old, m_old, p_new)
        _weighted_values(p_old, vs, o_ref, rows, shared_kv=shared_kv, dv=dv)

    @pl.when(g % 2 == 0)
    def _():
        tick(s_a, m_a, s_b, m_b, p_b, p_a)

    @pl.when(g % 2 == 1)
    def _():
        tick(s_b, m_b, s_a, m_a, p_a, p_b)


def _attn_ctx_kernel(q_ref, kt_ref, v_ref, o_ref, *, shared_kv, dk, dv):
    for hh in range(2):
        kv = 0 if shared_kv else hh
        s = _dot(q_ref[0, :, hh * dk:(hh + 1) * dk], kt_ref[0, kv])
        p = jnp.exp2(s - s.max(axis=-1, keepdims=True)).astype(BF16)
        o = _dot(p, v_ref[0, kv])
        o_ref[0, :, hh * dv:(hh + 1) * dv] = (o[:, :dv] / o[:, dv:]).astype(BF16)


def _attention(q, kt_lat, kt_ctx, v_lat, v_ctx, *, heads, kv_heads, dk, dv, name):
    bsz, lq, _ = q.shape
    shared_kv = kv_heads < heads
    group = heads // kv_heads
    n_pairs = heads // 2
    n_qb = lq // ATTN_TQ
    n_steps = bsz * n_pairs * n_qb
    lag = 2
    kv_blk = 1 if shared_kv else 2
    n_lat, n_ctx = kt_lat.shape[-1], kt_ctx.shape[-1]

    def split(g):
        return g // (n_pairs * n_qb), (g // n_qb) % n_pairs, g % n_qb

    def q_map(g):
        b, p, i = split(jnp.minimum(g, n_steps - 1))
        return b, i, p

    def k_map(g):
        b, p, _ = split(jnp.minimum(g, n_steps - 1))
        return b, (2 * p) // group if shared_kv else p, 0, 0

    def v_map(g):
        b, p, _ = split(jnp.maximum(g - lag, 0))
        return b, (2 * p) // group if shared_kv else p, 0, 0

    def o_map(g):
        b, p, i = split(jnp.maximum(g - lag, 0))
        return b, i, p

    rows, keys = 2 * ATTN_TQ, n_lat + n_ctx
    return pl.pallas_call(
        functools.partial(_attn_pipe_kernel, shared_kv=shared_kv, dk=dk, dv=dv),
        grid=(n_steps + lag,),
        in_specs=[pl.BlockSpec((1, ATTN_TQ, 2 * dk), q_map),
                  pl.BlockSpec((1, kv_blk, dk, n_lat), k_map), pl.BlockSpec((1, kv_blk, dk, n_ctx), k_map),
                  pl.BlockSpec((1, kv_blk, n_lat, 2 * dv), v_map), pl.BlockSpec((1, kv_blk, n_ctx, 2 * dv), v_map)],
        out_specs=pl.BlockSpec((1, ATTN_TQ, 2 * dv), o_map),
        out_shape=jax.ShapeDtypeStruct((bsz, lq, heads * dv), BF16),
        scratch_shapes=[pltpu.VMEM((rows, keys), F32), pltpu.VMEM((rows, keys), F32),
                        pltpu.VMEM((rows, keys), BF16), pltpu.VMEM((rows, keys), BF16)]
                       + [pltpu.VMEM((rows, 1), F32)] * 2,
        compiler_params=_params(1),
        name=name,
    )(q, kt_lat, kt_ctx, v_lat, v_ctx)


def _attention_ctx(q, kt, v, *, heads, kv_heads, dk, dv, name):
    bsz, lq, _ = q.shape
    shared_kv = kv_heads < heads
    group = heads // kv_heads
    kv_blk = 1 if shared_kv else 2
    kmap = (lambda b, p: (b, (2 * p) // group, 0, 0)) if shared_kv else (lambda b, p: (b, p, 0, 0))
    return pl.pallas_call(
        functools.partial(_attn_ctx_kernel, shared_kv=shared_kv, dk=dk, dv=dv),
        grid=(bsz, heads // 2),
        in_specs=[pl.BlockSpec((1, lq, 2 * dk), lambda b, p: (b, 0, p)),
                  pl.BlockSpec((1, kv_blk, dk, kt.shape[-1]), kmap),
                  pl.BlockSpec((1, kv_blk, v.shape[2], 2 * dv), kmap)],
        out_specs=pl.BlockSpec((1, lq, 2 * dv), lambda b, p: (b, 0, p)),
        out_shape=jax.ShapeDtypeStruct((bsz, lq, heads * dv), BF16),
        compiler_params=_params(2),
        name=name,
    )(q, kt, v)


def _s5_kernel(u_ref, cin_ref, wb_ref, a_ref, apow_ref, wc_ref, yprev_ref, d_ref, wglu_ref,
               out_ref, cout_ref, bu_r, bu_i, xb, car, *, reverse, glu):
    @pl.when(pl.program_id(1) == 0)
    def _():
        car[...] = cin_ref[0]

    ub = u_ref[0].astype(BF16)
    n_tiles = S5_CH // LANE
    tile_states = S5_LANES // n_tiles
    for t in range(n_tiles):
        r = _dot(ub[:, t * LANE:(t + 1) * LANE], wb_ref[t])
        bu_r[:, t * tile_states:(t + 1) * tile_states] = r[:, :tile_states]
        bu_i[:, t * tile_states:(t + 1) * tile_states] = r[:, tile_states:]

    steps = range(S5_SUB_LEN - 1, -1, -1) if reverse else range(S5_SUB_LEN)
    subs = range(S5_SUB - 1, -1, -1) if reverse else range(S5_SUB)
    pack_rows = 2 * S5_SUB
    for c in range(S5_LANES // S5_LANE_CHUNK):
        ls = slice(c * S5_LANE_CHUNK, (c + 1) * S5_LANE_CHUNK)
        ar = jnp.broadcast_to(a_ref[0:1, ls], (S5_SUB, S5_LANE_CHUNK))
        ai = jnp.broadcast_to(a_ref[1:2, ls], (S5_SUB, S5_LANE_CHUNK))
        xr = jnp.zeros((S5_SUB, S5_LANE_CHUNK), F32)
        xi = jnp.zeros((S5_SUB, S5_LANE_CHUNK), F32)
        for i in steps:
            rows = slice(i * S5_SUB, (i + 1) * S5_SUB)
            xr, xi = (ar * xr - ai * xi + bu_r[rows, ls], ar * xi + ai * xr + bu_i[rows, ls])
            bu_r[rows, ls] = xr
            bu_i[rows, ls] = xi
        cr, ci = car[0:1, ls], car[1:2, ls]
        a_sub_r, a_sub_i = a_ref[2:3, ls], a_ref[3:4, ls]
        crs, cis = [None] * S5_SUB, [None] * S5_SUB
        for j in subs:
            crs[j], cis[j] = cr, ci
            cr, ci = (a_sub_r * cr - a_sub_i * ci + xr[j:j + 1], a_sub_r * ci + a_sub_i * cr + xi[j:j + 1])
        car[0:1, ls] = cr
        car[1:2, ls] = ci
        cmr = jnp.concatenate(crs * (pack_rows // S5_SUB), axis=0)
        cmi = jnp.concatenate(cis * (pack_rows // S5_SUB), axis=0)
        col = (c * S5_LANE_CHUNK // tile_states) * 2 * tile_states + (c * S5_LANE_CHUNK) % tile_states
        for g in range(S5_CHUNK // pack_rows):
            rows = slice(g * pack_rows, (g + 1) * pack_rows)
            pr, pi = apow_ref[0, rows, ls], apow_ref[1, rows, ls]
            xb[rows, col:col + S5_LANE_CHUNK] = (bu_r[rows, ls] + (pr * cmr - pi * cmi)).astype(BF16)
            xb[rows, col + tile_states:col + tile_states + S5_LANE_CHUNK] = (
                bu_i[rows, ls] + (pr * cmi + pi * cmr)).astype(BF16)
    cout_ref[0] = car[...]

    ys = [_dot(xb[:, t * 2 * tile_states:(t + 1) * 2 * tile_states], wc_ref[t]) for t in range(n_tiles)]
    y = jnp.concatenate(ys, axis=-1)
    if glu:
        y = y + yprev_ref[0] + d_ref[...] * u_ref[0]
        hg = _dot(jax.nn.gelu(y).astype(BF16), wglu_ref[...])
        out_ref[0] = (hg[:, :S5_CH] * jax.nn.sigmoid(hg[:, S5_CH:])).astype(out_ref.dtype)
    else:
        out_ref[0] = y


def _s5_scan(u, carry_in, sw, yprev, d, w_glu, *, reverse, glu):
    bsz, n_tok, _ = u.shape
    nc = n_tok // S5_CHUNK
    order = (lambda b, i: (b, nc - 1 - i, 0)) if reverse else (lambda b, i: (b, i, 0))
    tile_states = S5_LANES // (S5_CH // LANE)
    assert tile_states % S5_LANE_CHUNK == 0
    row_spec = pl.BlockSpec((1, S5_CHUNK, S5_CH), order)
    carry_spec = pl.BlockSpec((1, 2, S5_LANES), lambda b, i: (b, 0, 0))
    consts = [sw['wb'], sw['a'], sw['apow'], sw['wc']]
    out, carry = pl.pallas_call(
        functools.partial(_s5_kernel, reverse=reverse, glu=glu),
        grid=(bsz, nc),
        in_specs=[row_spec, carry_spec] + [_const_spec(a.shape) for a in consts]
                 + [row_spec, _const_spec(d.shape), _const_spec(w_glu.shape)],
        out_specs=[row_spec, carry_spec],
        out_shape=[jax.ShapeDtypeStruct((bsz, n_tok, S5_CH), BF16 if glu else F32),
                   jax.ShapeDtypeStruct((bsz, 2, S5_LANES), F32)],
        scratch_shapes=[pltpu.VMEM((S5_CHUNK, S5_LANES), F32), pltpu.VMEM((S5_CHUNK, S5_LANES), F32),
                        pltpu.VMEM((S5_CHUNK, 2 * S5_LANES), BF16), pltpu.VMEM((2, S5_LANES), F32)],
        compiler_params=_params(2),
        name="s5_bwd" if reverse else "s5_fwd",
    )(u, carry_in, *consts, yprev, d, w_glu)
    return out, carry


def _merge_kernel(x_ref, mod_ref, ya_ref, ys_ref, yc_ref, gate_ref, wa_ref, ws_ref, wc_ref, wo_ref,
                  g_ref, b_ref, o_ref):
    gate = gate_ref[0]
    merged = (gate[:, :D_MODEL] * _dot(ya_ref[0], wa_ref[...])
              + gate[:, D_MODEL:2 * D_MODEL] * _dot(ys_ref[0], ws_ref[...])
              + gate[:, 2 * D_MODEL:] * _dot(yc_ref[0], wc_ref[...]))
    mix = _dot(merged.astype(BF16), wo_ref[...])
    y = DEEPNORM_ALPHA * x_ref[0] + mod_ref[0][2:3] * mix
    o_ref[0] = _layer_norm(y) * g_ref[...] + b_ref[...]


def _mlp_kernel(x_ref, mod_ref, wu_ref, wd_ref, g_ref, b_ref, o_ref):
    x = x_ref[0]
    mod = mod_ref[0]
    h = (_layer_norm(x) * (1.0 + mod[4:5]) + mod[3:4]).astype(BF16)
    up = jnp.square(jnp.maximum(_dot(h, wu_ref[...]), 0.0)).astype(BF16)
    y = DEEPNORM_ALPHA * x + mod[5:6] * _dot(up, wd_ref[...])
    o_ref[0] = _layer_norm(y) * g_ref[...] + b_ref[...]


def _token_call(kernel, name, x, mod, rows, consts):
    bsz, n_tok, _ = x.shape
    per_batch_mod = mod.shape[0] > 1
    rb = min(ROW_BLOCK, n_tok)

    def row_spec(width):
        return pl.BlockSpec((1, rb, width), lambda b, i: (b, i, 0))

    mod_spec = pl.BlockSpec((1, N_MOD, D_MODEL), (lambda b, i: (b, 0, 0)) if per_batch_mod else (lambda b, i: (0, 0, 0)))
    return pl.pallas_call(
        kernel,
        grid=(bsz, n_tok // rb),
        in_specs=[row_spec(D_MODEL), mod_spec] + [row_spec(r.shape[-1]) for r in rows]
                 + [_const_spec(a.shape) for a in consts],
        out_specs=row_spec(D_MODEL),
        out_shape=jax.ShapeDtypeStruct((bsz, n_tok, D_MODEL), F32),
        compiler_params=_params(2),
        name=name,
    )(x, mod, *rows, *consts)


def _axial_rope_tables(rows, dim):
    half = dim // 2
    inv = ROPE_THETA ** (-jnp.arange(0, half, 2, dtype=F32) / half)
    row = jnp.repeat(jnp.arange(rows, dtype=F32), GRID_W)
    col = jnp.tile(jnp.arange(GRID_W, dtype=F32), rows)
    ang_r = row[:, None] * inv
    ang_c = col[:, None] * inv
    ang = jnp.concatenate([ang_r, ang_r, ang_c, ang_c], axis=-1)
    return jnp.cos(ang), jnp.sin(ang)


def _rope_lane_tables(n_tok, dim, lane_off):
    cos, sin = _axial_rope_tables(n_tok // GRID_W, dim)
    reps = (LANE - lane_off) // dim if lane_off == 0 else 1
    cos_t = jnp.ones((n_tok, LANE), F32).at[:, lane_off:lane_off + reps * dim].set(jnp.tile(cos, (1, reps)))
    sin_t = jnp.zeros((n_tok, LANE), F32).at[:, lane_off:lane_off + reps * dim].set(jnp.tile(sin, (1, reps)))
    quarter = dim // 4
    first = (jnp.arange(LANE) % (2 * quarter)) < quarter
    return cos_t, jnp.where(first, -sin_t, 0.0), jnp.where(first, 0.0, sin_t)


def _block_diag(blocks):
    n, r, c = blocks.shape
    eye = jnp.eye(n, dtype=blocks.dtype)
    return (eye[:, None, :, None] * blocks[:, :, None, :]).reshape(n * r, n * c)


def _s5_discretize(a_re, a_im, log_dt, b_re, b_im):
    dt = jnp.exp(log_dt)[:, None]
    mag = jnp.exp(a_re * dt)
    abar_r = mag * jnp.cos(a_im * dt)
    abar_i = mag * jnp.sin(a_im * dt)
    den = a_re * a_re + a_im * a_im
    nr = abar_r - 1.0
    coef_r = (nr * a_re + abar_i * a_im) / den
    coef_i = (abar_i * a_re - nr * a_im) / den
    bbar_r = coef_r[..., None] * b_re - coef_i[..., None] * b_im
    bbar_i = coef_r[..., None] * b_im + coef_i[..., None] * b_re
    return abar_r, abar_i, bbar_r, bbar_i


def _s5_weights(a_re, a_im, log_dt, b_re, b_im, c_re, c_im, reverse):
    abar_r, abar_i, bbar_r, bbar_i = _s5_discretize(a_re, a_im, log_dt, b_re, b_im)
    groups_per_tile = LANE // S5_GROUP_CH
    n_tiles = S5_GROUPS // groups_per_tile

    def in_tile(bbar):
        blk = jnp.swapaxes(bbar, 1, 2).reshape(n_tiles, groups_per_tile, S5_GROUP_CH, S5_STATE)
        return jax.vmap(_block_diag)(blk)

    def out_tile(cm):
        blk = jnp.swapaxes(cm, 1, 2).reshape(n_tiles, groups_per_tile, S5_STATE, S5_GROUP_CH)
        return jax.vmap(_block_diag)(blk)

    wb = jnp.concatenate([in_tile(bbar_r), in_tile(bbar_i)], axis=-1).astype(BF16)
    wc = jnp.concatenate([out_tile(c_re), -out_tile(c_im)], axis=1).astype(BF16)

    def step(carry, _):
        pr, pi = carry
        nxt = (pr * abar_r - pi * abar_i, pr * abar_i + pi * abar_r)
        return nxt, nxt
    _, (pows_r, pows_i) = lax.scan(step, (jnp.ones_like(abar_r), jnp.zeros_like(abar_r)), None, length=S5_SUB_LEN)
    pows_r = pows_r.reshape(S5_SUB_LEN, S5_LANES)
    pows_i = pows_i.reshape(S5_SUB_LEN, S5_LANES)
    if reverse:
        pows_r, pows_i = pows_r[::-1], pows_i[::-1]
    sub_r = pows_r[0] if reverse else pows_r[-1]
    sub_i = pows_i[0] if reverse else pows_i[-1]
    a = jnp.stack([abar_r.reshape(-1), abar_i.reshape(-1), sub_r, sub_i])
    apow = jnp.repeat(jnp.stack([pows_r, pows_i]), S5_SUB, axis=1)
    return {'wb': wb, 'a': a, 'apow': apow, 'wc': wc}


def _interleave(t):
    b, n, w = t.shape
    return t.reshape(b, n // S5_CHUNK, S5_SUB, S5_SUB_LEN, w).swapaxes(2, 3).reshape(b, n, w)


def _deinterleave(t):
    b, n, w = t.shape
    return t.reshape(b, n // S5_CHUNK, S5_SUB_LEN, S5_SUB, w).swapaxes(2, 3).reshape(b, n, w)


def _layer_weights(l, w_in, a_q_gain, a_k_gain, c_q_a_gain, c_kv_a_gain, c_w_qb, c_w_kvb):
    w = w_in[l]
    ckr = jnp.zeros((D_MODEL, LANE), F32).at[:, C_NOPE:C_QK_DIM].set(w[:, OFF_CKR:OFF_U])
    w_re = jnp.concatenate([w[:, OFF_AK:OFF_CKR], ckr, w[:, OFF_U:]], axis=1).astype(BF16)
    assert w_re.shape[1] == P_COLS
    qb = c_w_qb[l].reshape(C_Q_RANK, C_HEADS, C_QK_DIM)
    qb = jnp.pad(qb, ((0, 0), (0, 0), (0, C_HEAD_PAD - C_QK_DIM))).reshape(C_Q_RANK, C_HEADS * C_HEAD_PAD)
    kvb = c_w_kvb[l].reshape(C_KV_RANK, C_HEADS, C_NOPE + C_VDIM)
    w_kc = jnp.pad(kvb[:, :, :C_NOPE], ((0, 0), (0, 0), (0, C_HEAD_PAD - C_NOPE)))
    w_kc = w_kc.reshape(C_KV_RANK, C_HEADS * C_HEAD_PAD)
    w_vc = kvb[:, :, C_NOPE:].reshape(C_KV_RANK, C_HEADS * C_VDIM)
    heads_per_tile = LANE // A_HEAD_DIM
    ones = _block_diag(jnp.full((heads_per_tile, A_HEAD_DIM, A_HEAD_DIM), 1.0 / A_HEAD_DIM, F32))
    return {
        'w_in': w_re, 'w_kc': w_kc.astype(BF16), 'w_vc': w_vc.astype(BF16), 'w_qb': qb.astype(BF16),
        'ones': jnp.concatenate([ones, ones], axis=0).astype(BF16),
        'gk': jnp.tile(a_k_gain[l], heads_per_tile)[None],
        'gq': jnp.tile(a_q_gain[l], heads_per_tile)[None] * (A_HEAD_DIM ** -0.5 * LOG2_E),
        'gkv': c_kv_a_gain[l][None], 'gcq': c_q_a_gain[l][None],
    }


def kernel(x, c, ctx, c_ctx, w_mod, b_mod, w_in, a_q_gain, a_k_gain, c_q_a_gain, c_kv_a_gain, c_w_qb, c_w_kvb, s5_a_re, s5_a_im, s5_log_dt, s5_b_re, s5_b_im, s5_c_re, s5_c_im, s5_d, s5_w_glu, w_branch_a, w_branch_s5, w_branch_c, w_out, ln1_g, ln1_b, w_up, w_down, ln2_g, ln2_b):
    bsz, seq, _ = x.shape
    n_ctx = ctx.shape[1]
    tabs_lat = _rope_lane_tables(seq, A_HEAD_DIM, 0) + _rope_lane_tables(seq, C_ROPE, C_NOPE)
    c_all = jnp.zeros((8, D_MODEL), F32).at[:bsz].set(c).at[bsz].set(c_ctx)
    zero_carry = jnp.zeros((bsz, 2, S5_LANES), F32)

    for l in range(DEPTH):
        last = l == DEPTH - 1
        lw = _layer_weights(l, w_in, a_q_gain, a_k_gain, c_q_a_gain, c_kv_a_gain, c_w_qb, c_w_kvb)
        sw = [_s5_weights(s5_a_re[l, dr], s5_a_im[l, dr], s5_log_dt[l, dr], s5_b_re[l, dr], s5_b_im[l, dr],
                          s5_c_re[l, dr], s5_c_im[l, dr], dr == 1) for dr in range(2)]
        d_row = s5_d[l][None]
        w_glu = s5_w_glu[l].astype(BF16)
        merge_w = [w_branch_a[l].astype(BF16), w_branch_s5[l].astype(BF16), w_branch_c[l].astype(BF16),
                   w_out[l].astype(BF16), ln1_g[l][None], ln1_b[l][None]]
        mlp_w = [w_up[l].astype(BF16), w_down[l].astype(BF16), ln2_g[l][None], ln2_b[l][None]]

        mod = _modulation(c_all, w_mod, b_mod, l).reshape(8, N_MOD, D_MODEL)
        mod_lat, mod_ctx = mod[:bsz], mod[bsz:bsz + 1]

        ka_c, va_c, kc_c, vc_c, u_c, qa_c, qc_c, gate_c = _inproj(ctx, mod_ctx, lw, tabs_lat, rope=False)
        ka, va, kc, vc, u, qa, qc, gate = _inproj(x, mod_lat, lw, tabs_lat, rope=True)

        u_c, u = _interleave(u_c), _interleave(u)
        yf_c, carry_f = _s5_scan(u_c, zero_carry, sw[0], u_c, d_row, w_glu, reverse=False, glu=False)
        yf, _ = _s5_scan(u, carry_f, sw[0], u, d_row, w_glu, reverse=False, glu=False)
        ys_c, carry_b = _s5_scan(u_c, zero_carry, sw[1], yf_c, d_row, w_glu, reverse=True, glu=True)
        ys, _ = _s5_scan(u, carry_b, sw[1], yf, d_row, w_glu, reverse=True, glu=True)
        ys_c, ys = _deinterleave(ys_c), _deinterleave(ys)

        ya = _attention(qa, ka, ka_c, va, va_c, heads=A_HEADS, kv_heads=A_KV_HEADS,
                        dk=A_HEAD_DIM, dv=A_HEAD_DIM, name="attn_a")
        yc = _attention(qc, kc, kc_c, vc, vc_c, heads=C_HEADS, kv_heads=C_HEADS,
                        dk=C_HEAD_PAD, dv=C_VDIM, name="attn_c")
        x_mid = _token_call(_merge_kernel, "merge", x, mod_lat, [ya, ys, yc, gate], merge_w)
        x_next = _token_call(_mlp_kernel, "mlp", x_mid, mod_lat, [], mlp_w)

        if not last:
            ya_c = _attention_ctx(qa_c, ka_c, va_c, heads=A_HEADS, kv_heads=A_KV_HEADS,
                                  dk=A_HEAD_DIM, dv=A_HEAD_DIM, name="attn_a_ctx")
            yc_c = _attention_ctx(qc_c, kc_c, vc_c, heads=C_HEADS, kv_heads=C_HEADS,
                                  dk=C_HEAD_PAD, dv=C_VDIM, name="attn_c_ctx")
            ctx_mid = _token_call(_merge_kernel, "merge_ctx", ctx, mod_ctx, [ya_c, ys_c, yc_c, gate_c], merge_w)
            ctx = _token_call(_mlp_kernel, "mlp_ctx", ctx_mid, mod_ctx, [], mlp_w)
        x = x_next
    return x
```

```python
import functools
import math

import jax
import jax.numpy as jnp
from jax import lax
from jax.experimental import pallas as pl
from jax.experimental.pallas import tpu as pltpu

F32 = jnp.float32
BF16 = jnp.bfloat16

D_MODEL = 1024
DEPTH = 2
GRID_W = 64
ROPE_THETA = 10000.0
EPS = 1e-6

A_HEADS = 8
A_KV_HEADS = 2
A_HEAD_DIM = 64
A_WIDTH = A_HEADS * A_HEAD_DIM

S5_CH = 512
S5_GROUP_CH = 16
S5_GROUPS = S5_CH // S5_GROUP_CH
S5_STATE = 64
S5_LANES = S5_GROUPS * S5_STATE

C_HEADS = 8
C_NOPE = 64
C_ROPE = 32
C_VDIM = 64
C_Q_RANK = 768
C_KV_RANK = 256
C_QK_DIM = C_NOPE + C_ROPE
C_WIDTH = C_HEADS * C_VDIM

D_FF = 4 * D_MODEL
N_BRANCH = 3
N_MOD = 6
DEEPNORM_ALPHA = (2.0 * DEPTH) ** 0.25
LOG2_E = math.log2(math.e)

OFF_AK = 0
OFF_AV = OFF_AK + A_KV_HEADS * A_HEAD_DIM
OFF_CKV = OFF_AV + A_KV_HEADS * A_HEAD_DIM
OFF_CKR = OFF_CKV + C_KV_RANK
OFF_U = OFF_CKR + C_ROPE
OFF_AQ = OFF_U + S5_CH
OFF_CQ = OFF_AQ + A_WIDTH
OFF_GATE = OFF_CQ + C_Q_RANK
N_IN_COLS = OFF_GATE + N_BRANCH * D_MODEL

LANE = 128
C_HEAD_PAD = LANE

P_AK = 0
P_AV = P_AK + LANE
P_CKV = P_AV + LANE
P_CKR = P_CKV + C_KV_RANK
P_U = P_CKR + LANE
P_AQ = P_U + S5_CH
P_CQ = P_AQ + A_WIDTH
P_GATE = P_CQ + C_Q_RANK
P_COLS = P_GATE + N_BRANCH * D_MODEL

ROW_BLOCK = 512
ATTN_TQ = 256
ATTN_KEY_CHUNK = 1024
S5_CHUNK = 256
S5_SUB = 8
S5_SUB_LEN = S5_CHUNK // S5_SUB
S5_LANE_CHUNK = 512
VMEM_LIMIT = 56 * 1024 * 1024


def _sel_spec(arr, idx):
    nd = arr.ndim - len(idx)
    return pl.BlockSpec((None,) * len(idx) + tuple(arr.shape[len(idx):]), lambda *_: tuple(idx) + (0,) * nd,
                        pipeline_mode=pl.Buffered(1))


def _params(n_grid):
    return pltpu.CompilerParams(dimension_semantics=("arbitrary",) * n_grid,
                                vmem_limit_bytes=VMEM_LIMIT)


def _layer_norm(x):
    mu = jnp.mean(x, axis=-1, keepdims=True)
    xc = x - mu
    var = jnp.mean(xc * xc, axis=-1, keepdims=True)
    return xc * lax.rsqrt(var + EPS)


def _dot(a, b):
    return jnp.dot(a, b, preferred_element_type=F32)


def _mod_kernel(c_ref, w_ref, b_ref, o_ref):
    s = jax.nn.silu(c_ref[...]).astype(BF16)
    o_ref[...] = _dot(s, w_ref[...].astype(BF16)) + b_ref[...]


def _modulation(c_all, w_mod, b_mod, layer):
    n = N_MOD * D_MODEL
    bn = n // 4
    return pl.pallas_call(
        _mod_kernel,
        grid=(n // bn,),
        in_specs=[pl.BlockSpec((8, D_MODEL), lambda j: (0, 0)),
                  pl.BlockSpec((None, D_MODEL, bn), lambda j: (layer, 0, j)),
                  pl.BlockSpec((None, 1, bn), lambda j: (layer, 0, j))],
        out_specs=pl.BlockSpec((8, bn), lambda j: (0, j)),
        out_shape=jax.ShapeDtypeStruct((8, n), F32),
        compiler_params=_params(1),
        name="mod",
    )(c_all, w_mod, b_mod.reshape(DEPTH, 1, n))


def _rope(x, cos, sin_next, sin_prev, shift):
    n = x.shape[-1]
    return x * cos + pltpu.roll(x, n - shift, 1) * sin_next + pltpu.roll(x, shift, 1) * sin_prev


def _head_rms_norm(p, ones_ref, gain):
    sq = p * p
    hi = sq.astype(BF16)
    lo = (sq - hi.astype(F32)).astype(BF16)
    ms = _dot(jnp.concatenate([hi, lo], axis=1), ones_ref[...])
    return p * lax.rsqrt(ms + EPS) * gain


def _row_rms_norm(p, gain):
    return p * lax.rsqrt(jnp.mean(p * p, axis=-1, keepdims=True) + EPS) * gain


def _value_tile(v, head):
    half = LANE // 2
    t = v[:, (head // 2) * LANE:(head // 2 + 1) * LANE]
    if head % 2:
        t = pltpu.roll(t, half, 1)
    low = lax.broadcasted_iota(jnp.int32, t.shape, 1) < half
    return jnp.where(low, t, 1.0).astype(BF16)


def _inproj_kernel(x_ref, mod_ref, w_ref, wkc_ref, wvc_ref, wqb_ref, ones_ref,
                   gk_ref, gq_ref, gkv_ref, gcq_ref,
                   cosa_ref, sna_ref, spa_ref, cosc_ref, snc_ref, spc_ref,
                   ka_ref, va_ref, kc_ref, vc_ref, u_ref, qa_ref, qc_ref, gate_ref, *, rope):
    mod = mod_ref[0]
    h = (_layer_norm(x_ref[0]) * (1.0 + mod[1:2]) + mod[0:1]).astype(BF16)

    def proj(a, b):
        return _dot(h, w_ref[:, a:b])

    def rope_a(t):
        return _rope(t, cosa_ref[...], sna_ref[...], spa_ref[...], A_HEAD_DIM // 4) if rope else t

    def rope_c(t):
        return _rope(t, cosc_ref[...], snc_ref[...], spc_ref[...], C_ROPE // 4) if rope else t

    state = proj(P_AK, P_U)
    kt = rope_a(_head_rms_norm(state[:, P_AK:P_AV], ones_ref, gk_ref[...])).T.astype(BF16)
    v = state[:, P_AV:P_CKV]
    for hh in range(A_KV_HEADS):
        ka_ref[0, hh] = kt[hh * A_HEAD_DIM:(hh + 1) * A_HEAD_DIM, :]
        va_ref[0, hh] = _value_tile(v, hh)

    ckv = _row_rms_norm(state[:, P_CKV:P_CKR], gkv_ref[...]).astype(BF16)
    k_rope = rope_c(state[:, P_CKR:P_U])
    k_nope = _dot(ckv, wkc_ref[...])
    vc = _dot(ckv, wvc_ref[...])
    for hh in range(C_HEADS):
        kc_ref[0, hh] = (k_nope[:, hh * C_HEAD_PAD:(hh + 1) * C_HEAD_PAD] + k_rope).T.astype(BF16)
        vc_ref[0, hh] = _value_tile(vc, hh)

    u_ref[0] = proj(P_U, P_AQ)

    aq = proj(P_AQ, P_CQ)
    for t in range(A_WIDTH // LANE):
        q = _head_rms_norm(aq[:, t * LANE:(t + 1) * LANE], ones_ref, gq_ref[...])
        qa_ref[0, :, t * LANE:(t + 1) * LANE] = rope_a(q).astype(BF16)

    cq = _row_rms_norm(proj(P_CQ, P_GATE), gcq_ref[...]).astype(BF16)
    qc = _dot(cq, wqb_ref[...])
    for hh in range(C_HEADS):
        q = rope_c(qc[:, hh * C_HEAD_PAD:(hh + 1) * C_HEAD_PAD])
        qc_ref[0, :, hh * C_HEAD_PAD:(hh + 1) * C_HEAD_PAD] = (q * (C_QK_DIM ** -0.5 * LOG2_E)).astype(BF16)

    for t in range(N_BRANCH):
        a = P_GATE + t * D_MODEL
        gate_ref[0, :, t * D_MODEL:(t + 1) * D_MODEL] = jax.nn.sigmoid(proj(a, a + D_MODEL))


def _inproj(x, mod, lw, layer, tabs, rope):
    bsz, n_tok, _ = x.shape
    rb = min(ROW_BLOCK, n_tok)
    nb = n_tok // rb
    per_batch_mod = mod.shape[0] > 1

    def row_spec(width):
        return pl.BlockSpec((1, rb, width), lambda b, i: (b, i, 0))

    def head_spec(heads, width):
        return pl.BlockSpec((1, heads, rb, width), lambda b, i: (b, 0, i, 0))

    tab_spec = pl.BlockSpec((rb, LANE), lambda b, i: (i, 0))
    mod_spec = pl.BlockSpec((1, N_MOD, D_MODEL), (lambda b, i: (b, 0, 0)) if per_batch_mod else (lambda b, i: (0, 0, 0)))
    consts = [lw['w_in'], lw['w_kc'], lw['w_vc'], lw['w_qb'], lw['ones'],
              lw['gk'], lw['gq'], lw['gkv'], lw['gcq']]
    def head_t_spec(heads, width):
        return pl.BlockSpec((1, heads, width, rb), lambda b, i: (b, 0, 0, i))

    out_shape = [
        jax.ShapeDtypeStruct((bsz, A_KV_HEADS, A_HEAD_DIM, n_tok), BF16),
        jax.ShapeDtypeStruct((bsz, A_KV_HEADS, n_tok, 2 * A_HEAD_DIM), BF16),
        jax.ShapeDtypeStruct((bsz, C_HEADS, C_HEAD_PAD, n_tok), BF16),
        jax.ShapeDtypeStruct((bsz, C_HEADS, n_tok, 2 * C_VDIM), BF16),
        jax.ShapeDtypeStruct((bsz, n_tok, S5_CH), F32),
        jax.ShapeDtypeStruct((bsz, n_tok, A_WIDTH), BF16),
        jax.ShapeDtypeStruct((bsz, n_tok, C_HEADS * C_HEAD_PAD), BF16),
        jax.ShapeDtypeStruct((bsz, n_tok, N_BRANCH * D_MODEL), F32),
    ]
    out_specs = [head_t_spec(A_KV_HEADS, A_HEAD_DIM), head_spec(A_KV_HEADS, 2 * A_HEAD_DIM),
                 head_t_spec(C_HEADS, C_HEAD_PAD), head_spec(C_HEADS, 2 * C_VDIM),
                 row_spec(S5_CH), row_spec(A_WIDTH), row_spec(C_HEADS * C_HEAD_PAD),
                 row_spec(N_BRANCH * D_MODEL)]
    return pl.pallas_call(
        functools.partial(_inproj_kernel, rope=rope),
        grid=(bsz, nb),
        in_specs=[row_spec(D_MODEL), mod_spec] + [_sel_spec(a, (layer,)) for a in consts] + [tab_spec] * 6,
        out_specs=out_specs,
        out_shape=out_shape,
        compiler_params=_params(2),
        name="inproj",
    )(x, mod, *consts, *tabs)


def _scores(q_ref, q_rows, kt_refs, s_ref, m_ref, *, shared_kv, dk):
    tq = ATTN_TQ
    q = q_ref[0, q_rows, :]
    if shared_kv:
        lhs = [(slice(0, 2 * tq), jnp.concatenate([q[:, :dk], q[:, dk:]], axis=0), 0)]
    else:
        lhs = [(slice(hh * tq, (hh + 1) * tq), q[:, hh * dk:(hh + 1) * dk], hh) for hh in range(2)]
    for rows, qh, hh in lhs:
        m = None
        off = 0
        for kt_ref in kt_refs:
            n = kt_ref.shape[-1]
            for c0 in range(0, n, ATTN_KEY_CHUNK):
                c1 = min(c0 + ATTN_KEY_CHUNK, n)
                s = _dot(qh, kt_ref[0, hh, :, c0:c1])
                s_ref[rows, off + c0:off + c1] = s
                mc = s.max(axis=-1, keepdims=True)
                m = mc if m is None else jnp.maximum(m, mc)
            off += n
        m_ref[rows] = m


def _exp_scores(s_ref, m_ref, p_ref):
    m = m_ref[...]
    n = s_ref.shape[-1]
    for c0 in range(0, n, ATTN_KEY_CHUNK):
        c1 = min(c0 + ATTN_KEY_CHUNK, n)
        p_ref[:, c0:c1] = jnp.exp2(s_ref[:, c0:c1] - m).astype(BF16)


def _weighted_values(p_ref, v_refs, o_ref, o_rows, *, shared_kv, dv):
    tq = ATTN_TQ
    for hh in range(2):
        rows = slice(hh * tq, (hh + 1) * tq)
        o = None
        off = 0
        for v_ref in v_refs:
            n = v_ref.shape[2]
            part = _dot(p_ref[rows, off:off + n], v_ref[0, 0 if shared_kv else hh])
            o = part if o is None else o + part
            off += n
        o_ref[0, o_rows, hh * dv:(hh + 1) * dv] = (o[:, :dv] / o[:, dv:]).astype(BF16)


def _attn_pipe_kernel(q_ref, kt_lat, kt_ctx, v_lat, v_ctx, o_ref,
                      s_a, s_b, p_a, p_b, m_a, m_b, *, shared_kv, dk, dv):
    g = pl.program_id(0)

    @pl.when(g == 0)
    def _():
        s_b[...] = jnp.zeros_like(s_b)
        m_b[...] = jnp.zeros_like(m_b)
        p_a[...] = jnp.ones_like(p_a)

    kts, vs = [kt_lat, kt_ctx], [v_lat, v_ctx]
    rows = slice(0, ATTN_TQ)

    def tick(s_new, m_new, s_old, m_old, p_new, p_old):
        _scores(q_ref, rows, kts, s_new, m_new, shared_kv=shared_kv, dk=dk)
        _exp_scores(s_old, m_old, p_new)
        _weighted_values(p_old, vs, o_ref, rows, shared_kv=shared_kv, dv=dv)

    @pl.when(g % 2 == 0)
    def _():
        tick(s_a, m_a, s_b, m_b, p_b, p_a)

    @pl.when(g % 2 == 1)
    def _():
        tick(s_b, m_b, s_a, m_a, p_a, p_b)


def _attn_ctx_kernel(q_ref, kt_ref, v_ref, o_ref, *, shared_kv, dk, dv):
    for hh in range(2):
        kv = 0 if shared_kv else hh
        s = _dot(q_ref[0, :, hh * dk:(hh + 1) * dk], kt_ref[0, kv])
        p = jnp.exp2(s - s.max(axis=-1, keepdims=True)).astype(BF16)
        o = _dot(p, v_ref[0, kv])
        o_ref[0, :, hh * dv:(hh + 1) * dv] = (o[:, :dv] / o[:, dv:]).astype(BF16)


def _attention(q, kt_lat, kt_ctx, v_lat, v_ctx, *, heads, kv_heads, dk, dv, name):
    bsz, lq, _ = q.shape
    shared_kv = kv_heads < heads
    group = heads // kv_heads
    n_pairs = heads // 2
    n_qb = lq // ATTN_TQ
    n_steps = bsz * n_pairs * n_qb
    lag = 2
    kv_blk = 1 if shared_kv else 2
    n_lat, n_ctx = kt_lat.shape[-1], kt_ctx.shape[-1]

    def split(g):
        return g // (n_pairs * n_qb), (g // n_qb) % n_pairs, g % n_qb

    def q_map(g):
        b, p, i = split(jnp.minimum(g, n_steps - 1))
        return b, i, p

    def k_map(g):
        b, p, _ = split(jnp.minimum(g, n_steps - 1))
        return b, (2 * p) // group if shared_kv else p, 0, 0

    def v_map(g):
        b, p, _ = split(jnp.maximum(g - lag, 0))
        return b, (2 * p) // group if shared_kv else p, 0, 0

    def o_map(g):
        b, p, i = split(jnp.maximum(g - lag, 0))
        return b, i, p

    rows, keys = 2 * ATTN_TQ, n_lat + n_ctx
    return pl.pallas_call(
        functools.partial(_attn_pipe_kernel, shared_kv=shared_kv, dk=dk, dv=dv),
        grid=(n_steps + lag,),
        in_specs=[pl.BlockSpec((1, ATTN_TQ, 2 * dk), q_map),
                  pl.BlockSpec((1, kv_blk, dk, n_lat), k_map), pl.BlockSpec((1, kv_blk, dk, n_ctx), k_map),
                  pl.BlockSpec((1, kv_blk, n_lat, 2 * dv), v_map), pl.BlockSpec((1, kv_blk, n_ctx, 2 * dv), v_map)],
        out_specs=pl.BlockSpec((1, ATTN_TQ, 2 * dv), o_map),
        out_shape=jax.ShapeDtypeStruct((bsz, lq, heads * dv), BF16),
        scratch_shapes=[pltpu.VMEM((rows, keys), F32), pltpu.VMEM((rows, keys), F32),
                        pltpu.VMEM((rows, keys), BF16), pltpu.VMEM((rows, keys), BF16)]
                       + [pltpu.VMEM((rows, 1), F32)] * 2,
        compiler_params=_params(1),
        name=name,
    )(q, kt_lat, kt_ctx, v_lat, v_ctx)


def _attention_ctx(q, kt, v, *, heads, kv_heads, dk, dv, name):
    bsz, lq, _ = q.shape
    shared_kv = kv_heads < heads
    group = heads // kv_heads
    kv_blk = 1 if shared_kv else 2
    kmap = (lambda b, p: (b, (2 * p) // group, 0, 0)) if shared_kv else (lambda b, p: (b, p, 0, 0))
    return pl.pallas_call(
        functools.partial(_attn_ctx_kernel, shared_kv=shared_kv, dk=dk, dv=dv),
        grid=(bsz, heads // 2),
        in_specs=[pl.BlockSpec((1, lq, 2 * dk), lambda b, p: (b, 0, p)),
                  pl.BlockSpec((1, kv_blk, dk, kt.shape[-1]), kmap),
                  pl.BlockSpec((1, kv_blk, v.shape[2], 2 * dv), kmap)],
        out_specs=pl.BlockSpec((1, lq, 2 * dv), lambda b, p: (b, 0, p)),
        out_shape=jax.ShapeDtypeStruct((bsz, lq, heads * dv), BF16),
        compiler_params=_params(2),
        name=name,
    )(q, kt, v)


def _s5_kernel(u_ref, cin_ref, wb_ref, a_ref, apow_ref, wc_ref, yprev_ref, d_ref, wglu_ref,
               out_ref, cout_ref, bu_r, bu_i, xb, car, *, reverse, glu):
    @pl.when(pl.program_id(1) == 0)
    def _():
        car[...] = cin_ref[0]

    ub = u_ref[0].astype(BF16)
    n_tiles = S5_CH // LANE
    tile_states = S5_LANES // n_tiles
    for t in range(n_tiles):
        r = _dot(ub[:, t * LANE:(t + 1) * LANE], wb_ref[t])
        bu_r[:, t * tile_states:(t + 1) * tile_states] = r[:, :tile_states]
        bu_i[:, t * tile_states:(t + 1) * tile_states] = r[:, tile_states:]

    steps = range(S5_SUB_LEN - 1, -1, -1) if reverse else range(S5_SUB_LEN)
    subs = range(S5_SUB - 1, -1, -1) if reverse else range(S5_SUB)
    pack_rows = 2 * S5_SUB
    for c in range(S5_LANES // S5_LANE_CHUNK):
        ls = slice(c * S5_LANE_CHUNK, (c + 1) * S5_LANE_CHUNK)
        ar = jnp.broadcast_to(a_ref[0:1, ls], (S5_SUB, S5_LANE_CHUNK))
        ai = jnp.broadcast_to(a_ref[1:2, ls], (S5_SUB, S5_LANE_CHUNK))
        xr = jnp.zeros((S5_SUB, S5_LANE_CHUNK), F32)
        xi = jnp.zeros((S5_SUB, S5_LANE_CHUNK), F32)
        for i in steps:
            rows = slice(i * S5_SUB, (i + 1) * S5_SUB)
            xr, xi = (ar * xr - ai * xi + bu_r[rows, ls], ar * xi + ai * xr + bu_i[rows, ls])
            bu_r[rows, ls] = xr
            bu_i[rows, ls] = xi
        cr, ci = car[0:1, ls], car[1:2, ls]
        a_sub_r, a_sub_i = a_ref[2:3, ls], a_ref[3:4, ls]
        crs, cis = [None] * S5_SUB, [None] * S5_SUB
        for j in subs:
            crs[j], cis[j] = cr, ci
            cr, ci = (a_sub_r * cr - a_sub_i * ci + xr[j:j + 1], a_sub_r * ci + a_sub_i * cr + xi[j:j + 1])
        car[0:1, ls] = cr
        car[1:2, ls] = ci
        cmr = jnp.concatenate(crs * (pack_rows // S5_SUB), axis=0)
        cmi = jnp.concatenate(cis * (pack_rows // S5_SUB), axis=0)
        col = (c * S5_LANE_CHUNK // tile_states) * 2 * tile_states + (c * S5_LANE_CHUNK) % tile_states
        for g in range(S5_CHUNK // pack_rows):
            rows = slice(g * pack_rows, (g + 1) * pack_rows)
            pr, pi = apow_ref[0, rows, ls], apow_ref[1, rows, ls]
            xb[rows, col:col + S5_LANE_CHUNK] = (bu_r[rows, ls] + (pr * cmr - pi * cmi)).astype(BF16)
            xb[rows, col + tile_states:col + tile_states + S5_LANE_CHUNK] = (
                bu_i[rows, ls] + (pr * cmi + pi * cmr)).astype(BF16)
    cout_ref[0] = car[...]

    ys = [_dot(xb[:, t * 2 * tile_states:(t + 1) * 2 * tile_states], wc_ref[t]) for t in range(n_tiles)]
    y = jnp.concatenate(ys, axis=-1)
    if glu:
        y = y + yprev_ref[0] + d_ref[...] * u_ref[0]
        hg = _dot(jax.nn.gelu(y).astype(BF16), wglu_ref[...])
        out_ref[0] = (hg[:, :S5_CH] * jax.nn.sigmoid(hg[:, S5_CH:])).astype(out_ref.dtype)
    else:
        out_ref[0] = y


def _s5_scan(u, carry_in, sw, layer, yprev, d, w_glu, *, reverse, glu):
    bsz, n_tok, _ = u.shape
    nc = n_tok // S5_CHUNK
    order = (lambda b, i: (b, nc - 1 - i, 0)) if reverse else (lambda b, i: (b, i, 0))
    tile_states = S5_LANES // (S5_CH // LANE)
    assert tile_states % S5_LANE_CHUNK == 0
    row_spec = pl.BlockSpec((1, S5_CHUNK, S5_CH), order)
    carry_spec = pl.BlockSpec((1, 2, S5_LANES), lambda b, i: (b, 0, 0))
    consts = [sw['wb'], sw['a'], sw['apow'], sw['wc']]
    out, carry = pl.pallas_call(
        functools.partial(_s5_kernel, reverse=reverse, glu=glu),
        grid=(bsz, nc),
        in_specs=[row_spec, carry_spec] + [_sel_spec(a, (layer, int(reverse))) for a in consts]
                 + [row_spec, _sel_spec(d, (layer,)), _sel_spec(w_glu, (layer,))],
        out_specs=[row_spec, carry_spec],
        out_shape=[jax.ShapeDtypeStruct((bsz, n_tok, S5_CH), BF16 if glu else F32),
                   jax.ShapeDtypeStruct((bsz, 2, S5_LANES), F32)],
        scratch_shapes=[pltpu.VMEM((S5_CHUNK, S5_LANES), F32), pltpu.VMEM((S5_CHUNK, S5_LANES), F32),
                        pltpu.VMEM((S5_CHUNK, 2 * S5_LANES), BF16), pltpu.VMEM((2, S5_LANES), F32)],
        compiler_params=_params(2),
        name="s5_bwd" if reverse else "s5_fwd",
    )(u, carry_in, *consts, yprev, d, w_glu)
    return out, carry


def _merge_kernel(x_ref, mod_ref, ya_ref, ys_ref, yc_ref, gate_ref, wa_ref, ws_ref, wc_ref, wo_ref,
                  g_ref, b_ref, o_ref):
    gate = gate_ref[0]
    merged = (gate[:, :D_MODEL] * _dot(ya_ref[0], wa_ref[...])
              + gate[:, D_MODEL:2 * D_MODEL] * _dot(ys_ref[0], ws_ref[...])
              + gate[:, 2 * D_MODEL:] * _dot(yc_ref[0], wc_ref[...]))
    mix = _dot(merged.astype(BF16), wo_ref[...])
    y = DEEPNORM_ALPHA * x_ref[0] + mod_ref[0][2:3] * mix
    o_ref[0] = _layer_norm(y) * g_ref[...] + b_ref[...]


def _mlp_kernel(x_ref, mod_ref, wu_ref, wd_ref, g_ref, b_ref, o_ref):
    x = x_ref[0]
    mod = mod_ref[0]
    h = (_layer_norm(x) * (1.0 + mod[4:5]) + mod[3:4]).astype(BF16)
    up = jnp.square(jnp.maximum(_dot(h, wu_ref[...]), 0.0)).astype(BF16)
    y = DEEPNORM_ALPHA * x + mod[5:6] * _dot(up, wd_ref[...])
    o_ref[0] = _layer_norm(y) * g_ref[...] + b_ref[...]


def _token_call(kernel, name, x, mod, rows, consts, layer):
    bsz, n_tok, _ = x.shape
    per_batch_mod = mod.shape[0] > 1
    rb = min(ROW_BLOCK, n_tok)

    def row_spec(width):
        return pl.BlockSpec((1, rb, width), lambda b, i: (b, i, 0))

    mod_spec = pl.BlockSpec((1, N_MOD, D_MODEL), (lambda b, i: (b, 0, 0)) if per_batch_mod else (lambda b, i: (0, 0, 0)))
    return pl.pallas_call(
        kernel,
        grid=(bsz, n_tok // rb),
        in_specs=[row_spec(D_MODEL), mod_spec] + [row_spec(r.shape[-1]) for r in rows]
                 + [_sel_spec(a, (layer,)) for a in consts],
        out_specs=row_spec(D_MODEL),
        out_shape=jax.ShapeDtypeStruct((bsz, n_tok, D_MODEL), F32),
        compiler_params=_params(2),
        name=name,
    )(x, mod, *rows, *consts)


def _axial_rope_tables(rows, dim):
    half = dim // 2
    inv = ROPE_THETA ** (-jnp.arange(0, half, 2, dtype=F32) / half)
    row = jnp.repeat(jnp.arange(rows, dtype=F32), GRID_W)
    col = jnp.tile(jnp.arange(GRID_W, dtype=F32), rows)
    ang_r = row[:, None] * inv
    ang_c = col[:, None] * inv
    ang = jnp.concatenate([ang_r, ang_r, ang_c, ang_c], axis=-1)
    return jnp.cos(ang), jnp.sin(ang)


def _rope_lane_tables(n_tok, dim, lane_off):
    cos, sin = _axial_rope_tables(n_tok // GRID_W, dim)
    reps = (LANE - lane_off) // dim if lane_off == 0 else 1
    cos_t = jnp.ones((n_tok, LANE), F32).at[:, lane_off:lane_off + reps * dim].set(jnp.tile(cos, (1, reps)))
    sin_t = jnp.zeros((n_tok, LANE), F32).at[:, lane_off:lane_off + reps * dim].set(jnp.tile(sin, (1, reps)))
    quarter = dim // 4
    first = (jnp.arange(LANE) % (2 * quarter)) < quarter
    return cos_t, jnp.where(first, -sin_t, 0.0), jnp.where(first, 0.0, sin_t)


def _block_diag(blocks):
    n, r, c = blocks.shape
    eye = jnp.eye(n, dtype=blocks.dtype)
    return (eye[:, None, :, None] * blocks[:, :, None, :]).reshape(n * r, n * c)


def _s5_discretize(a_re, a_im, log_dt, b_re, b_im):
    dt = jnp.exp(log_dt)[:, None]
    mag = jnp.exp(a_re * dt)
    abar_r = mag * jnp.cos(a_im * dt)
    abar_i = mag * jnp.sin(a_im * dt)
    den = a_re * a_re + a_im * a_im
    nr = abar_r - 1.0
    coef_r = (nr * a_re + abar_i * a_im) / den
    coef_i = (abar_i * a_re - nr * a_im) / den
    bbar_r = coef_r[..., None] * b_re - coef_i[..., None] * b_im
    bbar_i = coef_r[..., None] * b_im + coef_i[..., None] * b_re
    return abar_r, abar_i, bbar_r, bbar_i


def _s5_direction_weights(a_re, a_im, log_dt, b_re, b_im, c_re, c_im):
    abar_r, abar_i, bbar_r, bbar_i = _s5_discretize(a_re, a_im, log_dt, b_re, b_im)
    groups_per_tile = LANE // S5_GROUP_CH
    n_tiles = S5_GROUPS // groups_per_tile

    def in_tile(bbar):
        blk = jnp.swapaxes(bbar, 1, 2).reshape(n_tiles, groups_per_tile, S5_GROUP_CH, S5_STATE)
        return jax.vmap(_block_diag)(blk)

    def out_tile(cm):
        blk = jnp.swapaxes(cm, 1, 2).reshape(n_tiles, groups_per_tile, S5_STATE, S5_GROUP_CH)
        return jax.vmap(_block_diag)(blk)

    wb = jnp.concatenate([in_tile(bbar_r), in_tile(bbar_i)], axis=-1).astype(BF16)
    wc = jnp.concatenate([out_tile(c_re), -out_tile(c_im)], axis=1).astype(BF16)

    def step(carry, _):
        pr, pi = carry
        nxt = (pr * abar_r - pi * abar_i, pr * abar_i + pi * abar_r)
        return nxt, nxt
    _, (pows_r, pows_i) = lax.scan(step, (jnp.ones_like(abar_r), jnp.zeros_like(abar_r)), None, length=S5_SUB_LEN)
    pows = jnp.stack([pows_r.reshape(S5_SUB_LEN, S5_LANES), pows_i.reshape(S5_SUB_LEN, S5_LANES)])
    return wb, wc, jnp.stack([abar_r.reshape(-1), abar_i.reshape(-1)]), pows


def _s5_weights(a_re, a_im, log_dt, b_re, b_im, c_re, c_im):
    wb, wc, abar, pows = jax.vmap(jax.vmap(_s5_direction_weights))(a_re, a_im, log_dt, b_re, b_im, c_re, c_im)
    sub = pows[:, :, :, -1]
    pows = jnp.stack([pows[:, 0], pows[:, 1, :, ::-1]], axis=1)
    a = jnp.concatenate([abar, sub], axis=2)
    apow = jnp.repeat(pows, S5_SUB, axis=3)
    return {'wb': wb, 'a': a, 'apow': apow, 'wc': wc}


def _interleave(t):
    b, n, w = t.shape
    return t.reshape(b, n // S5_CHUNK, S5_SUB, S5_SUB_LEN, w).swapaxes(2, 3).reshape(b, n, w)


def _deinterleave(t):
    b, n, w = t.shape
    return t.reshape(b, n // S5_CHUNK, S5_SUB_LEN, S5_SUB, w).swapaxes(2, 3).reshape(b, n, w)


def _layer_weights(w_in, a_q_gain, a_k_gain, c_q_a_gain, c_kv_a_gain, c_w_qb, c_w_kvb):
    depth = w_in.shape[0]
    ckr = jnp.zeros((depth, D_MODEL, LANE), F32).at[:, :, C_NOPE:C_QK_DIM].set(w_in[:, :, OFF_CKR:OFF_U])
    w_re = jnp.concatenate([w_in[:, :, OFF_AK:OFF_CKR], ckr, w_in[:, :, OFF_U:]], axis=2).astype(BF16)
    assert w_re.shape[2] == P_COLS
    no_pad = ((0, 0), (0, 0), (0, 0))
    qb = c_w_qb.reshape(depth, C_Q_RANK, C_HEADS, C_QK_DIM)
    qb = jnp.pad(qb, no_pad + ((0, C_HEAD_PAD - C_QK_DIM),)).reshape(depth, C_Q_RANK, C_HEADS * C_HEAD_PAD)
    kvb = c_w_kvb.reshape(depth, C_KV_RANK, C_HEADS, C_NOPE + C_VDIM)
    w_kc = jnp.pad(kvb[..., :C_NOPE], no_pad + ((0, C_HEAD_PAD - C_NOPE),))
    w_kc = w_kc.reshape(depth, C_KV_RANK, C_HEADS * C_HEAD_PAD)
    w_vc = kvb[..., C_NOPE:].reshape(depth, C_KV_RANK, C_HEADS * C_VDIM)
    heads_per_tile = LANE // A_HEAD_DIM
    ones = _block_diag(jnp.full((heads_per_tile, A_HEAD_DIM, A_HEAD_DIM), 1.0 / A_HEAD_DIM, F32))
    ones = jnp.concatenate([ones, ones], axis=0).astype(BF16)
    return {
        'w_in': w_re, 'w_kc': w_kc.astype(BF16), 'w_vc': w_vc.astype(BF16), 'w_qb': qb.astype(BF16),
        'ones': jnp.broadcast_to(ones[None], (depth,) + ones.shape),
        'gk': jnp.tile(a_k_gain, (1, heads_per_tile))[:, None, :],
        'gq': jnp.tile(a_q_gain, (1, heads_per_tile))[:, None, :] * (A_HEAD_DIM ** -0.5 * LOG2_E),
        'gkv': c_kv_a_gain[:, None, :], 'gcq': c_q_a_gain[:, None, :],
    }


def kernel(x, c, ctx, c_ctx, w_mod, b_mod, w_in, a_q_gain, a_k_gain, c_q_a_gain, c_kv_a_gain, c_w_qb, c_w_kvb, s5_a_re, s5_a_im, s5_log_dt, s5_b_re, s5_b_im, s5_c_re, s5_c_im, s5_d, s5_w_glu, w_branch_a, w_branch_s5, w_branch_c, w_out, ln1_g, ln1_b, w_up, w_down, ln2_g, ln2_b):
    bsz, seq, _ = x.shape
    tabs_lat = _rope_lane_tables(seq, A_HEAD_DIM, 0) + _rope_lane_tables(seq, C_ROPE, C_NOPE)
    c_all = jnp.zeros((8, D_MODEL), F32).at[:bsz].set(c).at[bsz].set(c_ctx)
    zero_carry = jnp.zeros((bsz, 2, S5_LANES), F32)

    lw = _layer_weights(w_in, a_q_gain, a_k_gain, c_q_a_gain, c_kv_a_gain, c_w_qb, c_w_kvb)
    sw = _s5_weights(s5_a_re, s5_a_im, s5_log_dt, s5_b_re, s5_b_im, s5_c_re, s5_c_im)
    d_rows = s5_d[:, None, :]
    w_glu = s5_w_glu.astype(BF16)
    merge_w = [w_branch_a.astype(BF16), w_branch_s5.astype(BF16), w_branch_c.astype(BF16), w_out.astype(BF16),
               ln1_g[:, None, :], ln1_b[:, None, :]]
    mlp_w = [w_up.astype(BF16), w_down.astype(BF16), ln2_g[:, None, :], ln2_b[:, None, :]]

    for l in range(DEPTH):
        last = l == DEPTH - 1
        mod = _modulation(c_all, w_mod, b_mod, l).reshape(8, N_MOD, D_MODEL)
        mod_lat, mod_ctx = mod[:bsz], mod[bsz:bsz + 1]

        ka_c, va_c, kc_c, vc_c, u_c, qa_c, qc_c, gate_c = _inproj(ctx, mod_ctx, lw, l, tabs_lat, rope=False)
        ka, va, kc, vc, u, qa, qc, gate = _inproj(x, mod_lat, lw, l, tabs_lat, rope=True)

        u_c, u = _interleave(u_c), _interleave(u)
        yf_c, carry_f = _s5_scan(u_c, zero_carry, sw, l, u_c, d_rows, w_glu, reverse=False, glu=False)
        yf, _ = _s5_scan(u, carry_f, sw, l, u, d_rows, w_glu, reverse=False, glu=False)
        ys_c, carry_b = _s5_scan(u_c, zero_carry, sw, l, yf_c, d_rows, w_glu, reverse=True, glu=True)
        ys, _ = _s5_scan(u, carry_b, sw, l, yf, d_rows, w_glu, reverse=True, glu=True)
        ys_c, ys = _deinterleave(ys_c), _deinterleave(ys)

        ya = _attention(qa, ka, ka_c, va, va_c, heads=A_HEADS, kv_heads=A_KV_HEADS,
                        dk=A_HEAD_DIM, dv=A_HEAD_DIM, name="attn_a")
        yc = _attention(qc, kc, kc_c, vc, vc_c, heads=C_HEADS, kv_heads=C_HEADS,
                        dk=C_HEAD_PAD, dv=C_VDIM, name="attn_c")
        x_mid = _token_call(_merge_kernel, "merge", x, mod_lat, [ya, ys, yc, gate], merge_w, l)
        x_next = _token_call(_mlp_kernel, "mlp", x_mid, mod_lat, [], mlp_w, l)

        if not last:
            ya_c = _attention_ctx(qa_c, ka_c, va_c, heads=A_HEADS, kv_heads=A_KV_HEADS,
                                  dk=A_HEAD_DIM, dv=A_HEAD_DIM, name="attn_a_ctx")
            yc_c = _attention_ctx(qc_c, kc_c, vc_c, heads=C_HEADS, kv_heads=C_HEADS,
                                  dk=C_HEAD_PAD, dv=C_VDIM, name="attn_c_ctx")
            ctx_mid = _token_call(_merge_kernel, "merge_ctx", ctx, mod_ctx, [ya_c, ys_c, yc_c, gate_c], merge_w, l)
            ctx = _token_call(_mlp_kernel, "mlp_ctx", ctx_mid, mod_ctx, [], mlp_w, l)
        x = x_next
    return x
```

```python
import functools
import math

import jax
import jax.numpy as jnp
from jax import lax
from jax.experimental import pallas as pl
from jax.experimental.pallas import tpu as pltpu

F32 = jnp.float32
BF16 = jnp.bfloat16

D_MODEL = 1024
DEPTH = 2
GRID_W = 64
ROPE_THETA = 10000.0
EPS = 1e-6

A_HEADS = 8
A_KV_HEADS = 2
A_HEAD_DIM = 64
A_WIDTH = A_HEADS * A_HEAD_DIM

S5_CH = 512
S5_GROUP_CH = 16
S5_GROUPS = S5_CH // S5_GROUP_CH
S5_STATE = 64
S5_LANES = S5_GROUPS * S5_STATE

C_HEADS = 8
C_NOPE = 64
C_ROPE = 32
C_VDIM = 64
C_Q_RANK = 768
C_KV_RANK = 256
C_QK_DIM = C_NOPE + C_ROPE
C_WIDTH = C_HEADS * C_VDIM

D_FF = 4 * D_MODEL
N_BRANCH = 3
N_MOD = 6
DEEPNORM_ALPHA = (2.0 * DEPTH) ** 0.25
LOG2_E = math.log2(math.e)

OFF_AK = 0
OFF_AV = OFF_AK + A_KV_HEADS * A_HEAD_DIM
OFF_CKV = OFF_AV + A_KV_HEADS * A_HEAD_DIM
OFF_CKR = OFF_CKV + C_KV_RANK
OFF_U = OFF_CKR + C_ROPE
OFF_AQ = OFF_U + S5_CH
OFF_CQ = OFF_AQ + A_WIDTH
OFF_GATE = OFF_CQ + C_Q_RANK
N_IN_COLS = OFF_GATE + N_BRANCH * D_MODEL

LANE = 128
C_HEAD_PAD = LANE

P_AK = 0
P_AV = P_AK + LANE
P_CKV = P_AV + LANE
P_CKR = P_CKV + C_KV_RANK
P_U = P_CKR + LANE
P_AQ = P_U + S5_CH
P_CQ = P_AQ + A_WIDTH
P_GATE = P_CQ + C_Q_RANK
P_COLS = P_GATE + N_BRANCH * D_MODEL

ROW_BLOCK = 512
INPROJ_ROW_BLOCK = 1024
ATTN_TQ = 256
ATTN_KEY_CHUNK = 1024
S5_CHUNK = 256
S5_SUB = 8
S5_SUB_LEN = S5_CHUNK // S5_SUB
S5_LANE_CHUNK = 512
VMEM_LIMIT = 56 * 1024 * 1024


def _sel_spec(arr, idx):
    nd = arr.ndim - len(idx)
    return pl.BlockSpec((None,) * len(idx) + tuple(arr.shape[len(idx):]), lambda *_: tuple(idx) + (0,) * nd,
                        pipeline_mode=pl.Buffered(1))


def _params(n_grid):
    return pltpu.CompilerParams(dimension_semantics=("arbitrary",) * n_grid,
                                vmem_limit_bytes=VMEM_LIMIT)


def _layer_norm(x):
    mu = jnp.mean(x, axis=-1, keepdims=True)
    xc = x - mu
    var = jnp.mean(xc * xc, axis=-1, keepdims=True)
    return xc * lax.rsqrt(var + EPS)


def _dot(a, b):
    return jnp.dot(a, b, preferred_element_type=F32)


def _mod_kernel(c_ref, w_ref, b_ref, o_ref):
    s = jax.nn.silu(c_ref[...]).astype(BF16)
    o_ref[...] = _dot(s, w_ref[...].astype(BF16)) + b_ref[...]


def _modulation(c_all, w_mod, b_mod, layer):
    n = N_MOD * D_MODEL
    bn = n // 4
    return pl.pallas_call(
        _mod_kernel,
        grid=(n // bn,),
        in_specs=[pl.BlockSpec((8, D_MODEL), lambda j: (0, 0)),
                  pl.BlockSpec((None, D_MODEL, bn), lambda j: (layer, 0, j)),
                  pl.BlockSpec((None, 1, bn), lambda j: (layer, 0, j))],
        out_specs=pl.BlockSpec((8, bn), lambda j: (0, j)),
        out_shape=jax.ShapeDtypeStruct((8, n), F32),
        compiler_params=_params(1),
        name="mod",
    )(c_all, w_mod, b_mod.reshape(DEPTH, 1, n))


def _rope(x, cos, sin_next, sin_prev, shift):
    n = x.shape[-1]
    return x * cos + pltpu.roll(x, n - shift, 1) * sin_next + pltpu.roll(x, shift, 1) * sin_prev


def _head_rms_norm(p, ones_ref, gain):
    sq = p * p
    hi = sq.astype(BF16)
    lo = (sq - hi.astype(F32)).astype(BF16)
    ms = _dot(jnp.concatenate([hi, lo], axis=1), ones_ref[...])
    return p * lax.rsqrt(ms + EPS) * gain


def _row_rms_norm(p, gain):
    return p * lax.rsqrt(jnp.mean(p * p, axis=-1, keepdims=True) + EPS) * gain


def _value_tile(v, head):
    half = LANE // 2
    t = v[:, (head // 2) * LANE:(head // 2 + 1) * LANE]
    if head % 2:
        t = pltpu.roll(t, half, 1)
    low = lax.broadcasted_iota(jnp.int32, t.shape, 1) < half
    return jnp.where(low, t, 1.0).astype(BF16)


def _inproj_kernel(x_ref, mod_ref, w_ref, wkc_ref, wvc_ref, wqb_ref, ones_ref,
                   gk_ref, gq_ref, gkv_ref, gcq_ref,
                   cosa_ref, sna_ref, spa_ref, cosc_ref, snc_ref, spc_ref,
                   ka_ref, va_ref, kc_ref, vc_ref, u_ref, qa_ref, qc_ref, *, rope):
    mod = mod_ref[0]
    h = (_layer_norm(x_ref[0]) * (1.0 + mod[1:2]) + mod[0:1]).astype(BF16)

    def proj(a, b):
        return _dot(h, w_ref[:, a:b])

    def rope_a(t):
        return _rope(t, cosa_ref[...], sna_ref[...], spa_ref[...], A_HEAD_DIM // 4) if rope else t

    def rope_c(t):
        return _rope(t, cosc_ref[...], snc_ref[...], spc_ref[...], C_ROPE // 4) if rope else t

    state = proj(P_AK, P_U)
    kt = rope_a(_head_rms_norm(state[:, P_AK:P_AV], ones_ref, gk_ref[...])).T.astype(BF16)
    v = state[:, P_AV:P_CKV]
    for hh in range(A_KV_HEADS):
        ka_ref[0, hh] = kt[hh * A_HEAD_DIM:(hh + 1) * A_HEAD_DIM, :]
        va_ref[0, hh] = _value_tile(v, hh)

    ckv = _row_rms_norm(state[:, P_CKV:P_CKR], gkv_ref[...]).astype(BF16)
    k_rope = rope_c(state[:, P_CKR:P_U])
    k_nope = _dot(ckv, wkc_ref[...])
    vc = _dot(ckv, wvc_ref[...])
    for hh in range(C_HEADS):
        kc_ref[0, hh] = (k_nope[:, hh * C_HEAD_PAD:(hh + 1) * C_HEAD_PAD] + k_rope).T.astype(BF16)
        vc_ref[0, hh] = _value_tile(vc, hh)

    u_ref[0] = proj(P_U, P_AQ)

    aq = proj(P_AQ, P_CQ)
    for t in range(A_WIDTH // LANE):
        q = _head_rms_norm(aq[:, t * LANE:(t + 1) * LANE], ones_ref, gq_ref[...])
        qa_ref[0, :, t * LANE:(t + 1) * LANE] = rope_a(q).astype(BF16)

    cq = _row_rms_norm(proj(P_CQ, P_GATE), gcq_ref[...]).astype(BF16)
    qc = _dot(cq, wqb_ref[...])
    for hh in range(C_HEADS):
        q = rope_c(qc[:, hh * C_HEAD_PAD:(hh + 1) * C_HEAD_PAD])
        qc_ref[0, :, hh * C_HEAD_PAD:(hh + 1) * C_HEAD_PAD] = (q * (C_QK_DIM ** -0.5 * LOG2_E)).astype(BF16)


def _inproj(x, mod, lw, layer, tabs, rope):
    bsz, n_tok, _ = x.shape
    rb = min(INPROJ_ROW_BLOCK, n_tok)
    nb = n_tok // rb
    per_batch_mod = mod.shape[0] > 1

    def row_spec(width):
        return pl.BlockSpec((1, rb, width), lambda b, i: (b, i, 0))

    def head_spec(heads, width):
        return pl.BlockSpec((1, heads, rb, width), lambda b, i: (b, 0, i, 0))

    tab_spec = pl.BlockSpec((rb, LANE), lambda b, i: (i, 0))
    mod_spec = pl.BlockSpec((1, N_MOD, D_MODEL), (lambda b, i: (b, 0, 0)) if per_batch_mod else (lambda b, i: (0, 0, 0)))
    consts = [lw['w_in'], lw['w_kc'], lw['w_vc'], lw['w_qb'], lw['ones'],
              lw['gk'], lw['gq'], lw['gkv'], lw['gcq']]
    def head_t_spec(heads, width):
        return pl.BlockSpec((1, heads, width, rb), lambda b, i: (b, 0, 0, i))

    out_shape = [
        jax.ShapeDtypeStruct((bsz, A_KV_HEADS, A_HEAD_DIM, n_tok), BF16),
        jax.ShapeDtypeStruct((bsz, A_KV_HEADS, n_tok, 2 * A_HEAD_DIM), BF16),
        jax.ShapeDtypeStruct((bsz, C_HEADS, C_HEAD_PAD, n_tok), BF16),
        jax.ShapeDtypeStruct((bsz, C_HEADS, n_tok, 2 * C_VDIM), BF16),
        jax.ShapeDtypeStruct((bsz, n_tok, S5_CH), F32),
        jax.ShapeDtypeStruct((bsz, n_tok, A_WIDTH), BF16),
        jax.ShapeDtypeStruct((bsz, n_tok, C_HEADS * C_HEAD_PAD), BF16),
    ]
    out_specs = [head_t_spec(A_KV_HEADS, A_HEAD_DIM), head_spec(A_KV_HEADS, 2 * A_HEAD_DIM),
                 head_t_spec(C_HEADS, C_HEAD_PAD), head_spec(C_HEADS, 2 * C_VDIM),
                 row_spec(S5_CH), row_spec(A_WIDTH), row_spec(C_HEADS * C_HEAD_PAD)]
    return pl.pallas_call(
        functools.partial(_inproj_kernel, rope=rope),
        grid=(bsz, nb),
        in_specs=[row_spec(D_MODEL), mod_spec] + [_sel_spec(a, (layer,)) for a in consts] + [tab_spec] * 6,
        out_specs=out_specs,
        out_shape=out_shape,
        compiler_params=_params(2),
        name="inproj",
    )(x, mod, *consts, *tabs)


def _scores(q_ref, q_rows, kt_refs, s_ref, m_ref, *, shared_kv, dk):
    tq = ATTN_TQ
    q = q_ref[0, q_rows, :]
    if shared_kv:
        lhs = [(slice(0, 2 * tq), jnp.concatenate([q[:, :dk], q[:, dk:]], axis=0), 0)]
    else:
        lhs = [(slice(hh * tq, (hh + 1) * tq), q[:, hh * dk:(hh + 1) * dk], hh) for hh in range(2)]
    for rows, qh, hh in lhs:
        m = None
        off = 0
        for kt_ref in kt_refs:
            n = kt_ref.shape[-1]
            for c0 in range(0, n, ATTN_KEY_CHUNK):
                c1 = min(c0 + ATTN_KEY_CHUNK, n)
                s = _dot(qh, kt_ref[0, hh, :, c0:c1])
                s_ref[rows, off + c0:off + c1] = s
                mc = s.max(axis=-1, keepdims=True)
                m = mc if m is None else jnp.maximum(m, mc)
            off += n
        m_ref[rows] = m


def _exp_scores(s_ref, m_ref, p_ref):
    m = m_ref[...]
    n = s_ref.shape[-1]
    for c0 in range(0, n, ATTN_KEY_CHUNK):
        c1 = min(c0 + ATTN_KEY_CHUNK, n)
        p_ref[:, c0:c1] = jnp.exp2(s_ref[:, c0:c1] - m).astype(BF16)


def _weighted_values(p_ref, v_refs, o_ref, o_rows, *, shared_kv, dv):
    tq = ATTN_TQ
    for hh in range(2):
        rows = slice(hh * tq, (hh + 1) * tq)
        o = None
        off = 0
        for v_ref in v_refs:
            n = v_ref.shape[2]
            part = _dot(p_ref[rows, off:off + n], v_ref[0, 0 if shared_kv else hh])
            o = part if o is None else o + part
            off += n
        o_ref[0, o_rows, hh * dv:(hh + 1) * dv] = (o[:, :dv] / o[:, dv:]).astype(BF16)


def _attn_pipe_kernel(q_ref, kt_lat, kt_ctx, v_lat, v_ctx, o_ref,
                      s_a, s_b, p_a, p_b, m_a, m_b, *, shared_kv, dk, dv):
    g = pl.program_id(0)

    @pl.when(g == 0)
    def _():
        s_b[...] = jnp.zeros_like(s_b)
        m_b[...] = jnp.zeros_like(m_b)
        p_a[...] = jnp.ones_like(p_a)

    kts, vs = [kt_lat, kt_ctx], [v_lat, v_ctx]
    rows = slice(0, ATTN_TQ)

    def tick(s_new, m_new, s_old, m_old, p_new, p_old):
        _scores(q_ref, rows, kts, s_new, m_new, shared_kv=shared_kv, dk=dk)
        _exp_scores(s_old, m_old, p_new)
        _weighted_values(p_old, vs, o_ref, rows, shared_kv=shared_kv, dv=dv)

    @pl.when(g % 2 == 0)
    def _():
        tick(s_a, m_a, s_b, m_b, p_b, p_a)

    @pl.when(g % 2 == 1)
    def _():
        tick(s_b, m_b, s_a, m_a, p_a, p_b)


def _attn_ctx_kernel(q_ref, kt_ref, v_ref, o_ref, *, shared_kv, dk, dv):
    for hh in range(2):
        kv = 0 if shared_kv else hh
        s = _dot(q_ref[0, :, hh * dk:(hh + 1) * dk], kt_ref[0, kv])
        p = jnp.exp2(s - s.max(axis=-1, keepdims=True)).astype(BF16)
        o = _dot(p, v_ref[0, kv])
        o_ref[0, :, hh * dv:(hh + 1) * dv] = (o[:, :dv] / o[:, dv:]).astype(BF16)


def _attention(q, kt_lat, kt_ctx, v_lat, v_ctx, *, heads, kv_heads, dk, dv, name):
    bsz, lq, _ = q.shape
    shared_kv = kv_heads < heads
    group = heads // kv_heads
    n_pairs = heads // 2
    n_qb = lq // ATTN_TQ
    n_steps = bsz * n_pairs * n_qb
    lag = 2
    kv_blk = 1 if shared_kv else 2
    n_lat, n_ctx = kt_lat.shape[-1], kt_ctx.shape[-1]

    def split(g):
        return g // (n_pairs * n_qb), (g // n_qb) % n_pairs, g % n_qb

    def q_map(g):
        b, p, i = split(jnp.minimum(g, n_steps - 1))
        return b, i, p

    def k_map(g):
        b, p, _ = split(jnp.minimum(g, n_steps - 1))
        return b, (2 * p) // group if shared_kv else p, 0, 0

    def v_map(g):
        b, p, _ = split(jnp.maximum(g - lag, 0))
        return b, (2 * p) // group if shared_kv else p, 0, 0

    def o_map(g):
        b, p, i = split(jnp.maximum(g - lag, 0))
        return b, i, p

    rows, keys = 2 * ATTN_TQ, n_lat + n_ctx
    return pl.pallas_call(
        functools.partial(_attn_pipe_kernel, shared_kv=shared_kv, dk=dk, dv=dv),
        grid=(n_steps + lag,),
        in_specs=[pl.BlockSpec((1, ATTN_TQ, 2 * dk), q_map),
                  pl.BlockSpec((1, kv_blk, dk, n_lat), k_map), pl.BlockSpec((1, kv_blk, dk, n_ctx), k_map),
                  pl.BlockSpec((1, kv_blk, n_lat, 2 * dv), v_map), pl.BlockSpec((1, kv_blk, n_ctx, 2 * dv), v_map)],
        out_specs=pl.BlockSpec((1, ATTN_TQ, 2 * dv), o_map),
        out_shape=jax.ShapeDtypeStruct((bsz, lq, heads * dv), BF16),
        scratch_shapes=[pltpu.VMEM((rows, keys), F32), pltpu.VMEM((rows, keys), F32),
                        pltpu.VMEM((rows, keys), BF16), pltpu.VMEM((rows, keys), BF16)]
                       + [pltpu.VMEM((rows, 1), F32)] * 2,
        compiler_params=_params(1),
        name=name,
    )(q, kt_lat, kt_ctx, v_lat, v_ctx)


def _attention_ctx(q, kt, v, *, heads, kv_heads, dk, dv, name):
    bsz, lq, _ = q.shape
    shared_kv = kv_heads < heads
    group = heads // kv_heads
    kv_blk = 1 if shared_kv else 2
    kmap = (lambda b, p: (b, (2 * p) // group, 0, 0)) if shared_kv else (lambda b, p: (b, p, 0, 0))
    return pl.pallas_call(
        functools.partial(_attn_ctx_kernel, shared_kv=shared_kv, dk=dk, dv=dv),
        grid=(bsz, heads // 2),
        in_specs=[pl.BlockSpec((1, lq, 2 * dk), lambda b, p: (b, 0, p)),
                  pl.BlockSpec((1, kv_blk, dk, kt.shape[-1]), kmap),
                  pl.BlockSpec((1, kv_blk, v.shape[2], 2 * dv), kmap)],
        out_specs=pl.BlockSpec((1, lq, 2 * dv), lambda b, p: (b, 0, p)),
        out_shape=jax.ShapeDtypeStruct((bsz, lq, heads * dv), BF16),
        compiler_params=_params(2),
        name=name,
    )(q, kt, v)


def _s5_kernel(u_ref, cin_ref, wb_ref, a_ref, apow_ref, wc_ref, yprev_ref, d_ref, wglu_ref,
               out_ref, cout_ref, bu_r, bu_i, xb, car, *, reverse, glu):
    @pl.when(pl.program_id(1) == 0)
    def _():
        car[...] = cin_ref[0]

    ub = u_ref[0].astype(BF16)
    n_tiles = S5_CH // LANE
    tile_states = S5_LANES // n_tiles
    for t in range(n_tiles):
        r = _dot(ub[:, t * LANE:(t + 1) * LANE], wb_ref[t])
        bu_r[:, t * tile_states:(t + 1) * tile_states] = r[:, :tile_states]
        bu_i[:, t * tile_states:(t + 1) * tile_states] = r[:, tile_states:]

    steps = range(S5_SUB_LEN - 1, -1, -1) if reverse else range(S5_SUB_LEN)
    subs = range(S5_SUB - 1, -1, -1) if reverse else range(S5_SUB)
    pack_rows = 2 * S5_SUB
    for c in range(S5_LANES // S5_LANE_CHUNK):
        ls = slice(c * S5_LANE_CHUNK, (c + 1) * S5_LANE_CHUNK)
        ar = jnp.broadcast_to(a_ref[0:1, ls], (S5_SUB, S5_LANE_CHUNK))
        ai = jnp.broadcast_to(a_ref[1:2, ls], (S5_SUB, S5_LANE_CHUNK))
        xr = jnp.zeros((S5_SUB, S5_LANE_CHUNK), F32)
        xi = jnp.zeros((S5_SUB, S5_LANE_CHUNK), F32)
        for i in steps:
            rows = slice(i * S5_SUB, (i + 1) * S5_SUB)
            xr, xi = (ar * xr - ai * xi + bu_r[rows, ls], ar * xi + ai * xr + bu_i[rows, ls])
            bu_r[rows, ls] = xr
            bu_i[rows, ls] = xi
        cr, ci = car[0:1, ls], car[1:2, ls]
        a_sub_r, a_sub_i = a_ref[2:3, ls], a_ref[3:4, ls]
        crs, cis = [None] * S5_SUB, [None] * S5_SUB
        for j in subs:
            crs[j], cis[j] = cr, ci
            cr, ci = (a_sub_r * cr - a_sub_i * ci + xr[j:j + 1], a_sub_r * ci + a_sub_i * cr + xi[j:j + 1])
        car[0:1, ls] = cr
        car[1:2, ls] = ci
        cmr = jnp.concatenate(crs * (pack_rows // S5_SUB), axis=0)
        cmi = jnp.concatenate(cis * (pack_rows // S5_SUB), axis=0)
        col = (c * S5_LANE_CHUNK // tile_states) * 2 * tile_states + (c * S5_LANE_CHUNK) % tile_states
        for g in range(S5_CHUNK // pack_rows):
            rows = slice(g * pack_rows, (g + 1) * pack_rows)
            pr, pi = apow_ref[0, rows, ls], apow_ref[1, rows, ls]
            xb[rows, col:col + S5_LANE_CHUNK] = (bu_r[rows, ls] + (pr * cmr - pi * cmi)).astype(BF16)
            xb[rows, col + tile_states:col + tile_states + S5_LANE_CHUNK] = (
                bu_i[rows, ls] + (pr * cmi + pi * cmr)).astype(BF16)
    cout_ref[0] = car[...]

    ys = [_dot(xb[:, t * 2 * tile_states:(t + 1) * 2 * tile_states], wc_ref[t]) for t in range(n_tiles)]
    y = jnp.concatenate(ys, axis=-1)
    if glu:
        y = y + yprev_ref[0] + d_ref[...] * u_ref[0]
        hg = _dot(jax.nn.gelu(y).astype(BF16), wglu_ref[...])
        out_ref[0] = (hg[:, :S5_CH] * jax.nn.sigmoid(hg[:, S5_CH:])).astype(out_ref.dtype)
    else:
        out_ref[0] = y


def _s5_scan(u, carry_in, sw, layer, yprev, d, w_glu, *, reverse, glu):
    bsz, n_tok, _ = u.shape
    nc = n_tok // S5_CHUNK
    order = (lambda b, i: (b, nc - 1 - i, 0)) if reverse else (lambda b, i: (b, i, 0))
    tile_states = S5_LANES // (S5_CH // LANE)
    assert tile_states % S5_LANE_CHUNK == 0
    row_spec = pl.BlockSpec((1, S5_CHUNK, S5_CH), order)
    carry_spec = pl.BlockSpec((1, 2, S5_LANES), lambda b, i: (b, 0, 0))
    consts = [sw['wb'], sw['a'], sw['apow'], sw['wc']]
    out, carry = pl.pallas_call(
        functools.partial(_s5_kernel, reverse=reverse, glu=glu),
        grid=(bsz, nc),
        in_specs=[row_spec, carry_spec] + [_sel_spec(a, (layer, int(reverse))) for a in consts]
                 + [row_spec, _sel_spec(d, (layer,)), _sel_spec(w_glu, (layer,))],
        out_specs=[row_spec, carry_spec],
        out_shape=[jax.ShapeDtypeStruct((bsz, n_tok, S5_CH), BF16 if glu else F32),
                   jax.ShapeDtypeStruct((bsz, 2, S5_LANES), F32)],
        scratch_shapes=[pltpu.VMEM((S5_CHUNK, S5_LANES), F32), pltpu.VMEM((S5_CHUNK, S5_LANES), F32),
                        pltpu.VMEM((S5_CHUNK, 2 * S5_LANES), BF16), pltpu.VMEM((2, S5_LANES), F32)],
        compiler_params=_params(2),
        name="s5_bwd" if reverse else "s5_fwd",
    )(u, carry_in, *consts, yprev, d, w_glu)
    return out, carry


def _merge_kernel(x_ref, mod_ref, ya_ref, ys_ref, yc_ref, wg_ref, wa_ref, ws_ref, wc_ref, wo_ref,
                  g_ref, b_ref, o_ref):
    x = x_ref[0]
    mod = mod_ref[0]
    h = (_layer_norm(x) * (1.0 + mod[1:2]) + mod[0:1]).astype(BF16)
    merged = None
    for t, (y_ref, w_ref) in enumerate(((ya_ref, wa_ref), (ys_ref, ws_ref), (yc_ref, wc_ref))):
        gate = jax.nn.sigmoid(_dot(h, wg_ref[:, t * D_MODEL:(t + 1) * D_MODEL]))
        term = gate * _dot(y_ref[0], w_ref[...])
        merged = term if merged is None else merged + term
    mix = _dot(merged.astype(BF16), wo_ref[...])
    y = DEEPNORM_ALPHA * x + mod[2:3] * mix
    o_ref[0] = _layer_norm(y) * g_ref[...] + b_ref[...]


def _mlp_kernel(x_ref, mod_ref, wu_ref, wd_ref, g_ref, b_ref, o_ref):
    x = x_ref[0]
    mod = mod_ref[0]
    h = (_layer_norm(x) * (1.0 + mod[4:5]) + mod[3:4]).astype(BF16)
    up = jnp.square(jnp.maximum(_dot(h, wu_ref[...]), 0.0)).astype(BF16)
    y = DEEPNORM_ALPHA * x + mod[5:6] * _dot(up, wd_ref[...])
    o_ref[0] = _layer_norm(y) * g_ref[...] + b_ref[...]


def _token_call(kernel, name, x, mod, rows, consts, layer):
    bsz, n_tok, _ = x.shape
    per_batch_mod = mod.shape[0] > 1
    rb = min(ROW_BLOCK, n_tok)

    def row_spec(width):
        return pl.BlockSpec((1, rb, width), lambda b, i: (b, i, 0))

    mod_spec = pl.BlockSpec((1, N_MOD, D_MODEL), (lambda b, i: (b, 0, 0)) if per_batch_mod else (lambda b, i: (0, 0, 0)))
    return pl.pallas_call(
        kernel,
        grid=(bsz, n_tok // rb),
        in_specs=[row_spec(D_MODEL), mod_spec] + [row_spec(r.shape[-1]) for r in rows]
                 + [_sel_spec(a, (layer,)) for a in consts],
        out_specs=row_spec(D_MODEL),
        out_shape=jax.ShapeDtypeStruct((bsz, n_tok, D_MODEL), F32),
        compiler_params=_params(2),
        name=name,
    )(x, mod, *rows, *consts)


def _axial_rope_tables(rows, dim):
    half = dim // 2
    inv = ROPE_THETA ** (-jnp.arange(0, half, 2, dtype=F32) / half)
    row = jnp.repeat(jnp.arange(rows, dtype=F32), GRID_W)
    col = jnp.tile(jnp.arange(GRID_W, dtype=F32), rows)
    ang_r = row[:, None] * inv
    ang_c = col[:, None] * inv
    ang = jnp.concatenate([ang_r, ang_r, ang_c, ang_c], axis=-1)
    return jnp.cos(ang), jnp.sin(ang)


def _rope_lane_tables(n_tok, dim, lane_off):
    cos, sin = _axial_rope_tables(n_tok // GRID_W, dim)
    reps = (LANE - lane_off) // dim if lane_off == 0 else 1
    cos_t = jnp.ones((n_tok, LANE), F32).at[:, lane_off:lane_off + reps * dim].set(jnp.tile(cos, (1, reps)))
    sin_t = jnp.zeros((n_tok, LANE), F32).at[:, lane_off:lane_off + reps * dim].set(jnp.tile(sin, (1, reps)))
    quarter = dim // 4
    first = (jnp.arange(LANE) % (2 * quarter)) < quarter
    return cos_t, jnp.where(first, -sin_t, 0.0), jnp.where(first, 0.0, sin_t)


def _block_diag(blocks):
    n, r, c = blocks.shape
    eye = jnp.eye(n, dtype=blocks.dtype)
    return (eye[:, None, :, None] * blocks[:, :, None, :]).reshape(n * r, n * c)


def _s5_discretize(a_re, a_im, log_dt, b_re, b_im):
    dt = jnp.exp(log_dt)[:, None]
    mag = jnp.exp(a_re * dt)
    abar_r = mag * jnp.cos(a_im * dt)
    abar_i = mag * jnp.sin(a_im * dt)
    den = a_re * a_re + a_im * a_im
    nr = abar_r - 1.0
    coef_r = (nr * a_re + abar_i * a_im) / den
    coef_i = (abar_i * a_re - nr * a_im) / den
    bbar_r = coef_r[..., None] * b_re - coef_i[..., None] * b_im
    bbar_i = coef_r[..., None] * b_im + coef_i[..., None] * b_re
    return abar_r, abar_i, bbar_r, bbar_i


def _s5_direction_weights(a_re, a_im, log_dt, b_re, b_im, c_re, c_im):
    abar_r, abar_i, bbar_r, bbar_i = _s5_discretize(a_re, a_im, log_dt, b_re, b_im)
    groups_per_tile = LANE // S5_GROUP_CH
    n_tiles = S5_GROUPS // groups_per_tile

    def in_tile(bbar):
        blk = jnp.swapaxes(bbar, 1, 2).reshape(n_tiles, groups_per_tile, S5_GROUP_CH, S5_STATE)
        return jax.vmap(_block_diag)(blk)

    def out_tile(cm):
        blk = jnp.swapaxes(cm, 1, 2).reshape(n_tiles, groups_per_tile, S5_STATE, S5_GROUP_CH)
        return jax.vmap(_block_diag)(blk)

    wb = jnp.concatenate([in_tile(bbar_r), in_tile(bbar_i)], axis=-1).astype(BF16)
    wc = jnp.concatenate([out_tile(c_re), -out_tile(c_im)], axis=1).astype(BF16)

    def step(carry, _):
        pr, pi = carry
        nxt = (pr * abar_r - pi * abar_i, pr * abar_i + pi * abar_r)
        return nxt, nxt
    _, (pows_r, pows_i) = lax.scan(step, (jnp.ones_like(abar_r), jnp.zeros_like(abar_r)), None, length=S5_SUB_LEN)
    pows = jnp.stack([pows_r.reshape(S5_SUB_LEN, S5_LANES), pows_i.reshape(S5_SUB_LEN, S5_LANES)])
    return wb, wc, jnp.stack([abar_r.reshape(-1), abar_i.reshape(-1)]), pows


def _s5_weights(a_re, a_im, log_dt, b_re, b_im, c_re, c_im):
    wb, wc, abar, pows = jax.vmap(jax.vmap(_s5_direction_weights))(a_re, a_im, log_dt, b_re, b_im, c_re, c_im)
    sub = pows[:, :, :, -1]
    pows = jnp.stack([pows[:, 0], pows[:, 1, :, ::-1]], axis=1)
    a = jnp.concatenate([abar, sub], axis=2)
    apow = jnp.repeat(pows, S5_SUB, axis=3)
    return {'wb': wb, 'a': a, 'apow': apow, 'wc': wc}


def _interleave(t):
    b, n, w = t.shape
    return t.reshape(b, n // S5_CHUNK, S5_SUB, S5_SUB_LEN, w).swapaxes(2, 3).reshape(b, n, w)


def _deinterleave(t):
    b, n, w = t.shape
    return t.reshape(b, n // S5_CHUNK, S5_SUB_LEN, S5_SUB, w).swapaxes(2, 3).reshape(b, n, w)


def _layer_weights(w_in, a_q_gain, a_k_gain, c_q_a_gain, c_kv_a_gain, c_w_qb, c_w_kvb):
    depth = w_in.shape[0]
    ckr = jnp.zeros((depth, D_MODEL, LANE), F32).at[:, :, C_NOPE:C_QK_DIM].set(w_in[:, :, OFF_CKR:OFF_U])
    w_re = jnp.concatenate([w_in[:, :, OFF_AK:OFF_CKR], ckr, w_in[:, :, OFF_U:OFF_GATE]], axis=2).astype(BF16)
    assert w_re.shape[2] == P_GATE
    no_pad = ((0, 0), (0, 0), (0, 0))
    qb = c_w_qb.reshape(depth, C_Q_RANK, C_HEADS, C_QK_DIM)
    qb = jnp.pad(qb, no_pad + ((0, C_HEAD_PAD - C_QK_DIM),)).reshape(depth, C_Q_RANK, C_HEADS * C_HEAD_PAD)
    kvb = c_w_kvb.reshape(depth, C_KV_RANK, C_HEADS, C_NOPE + C_VDIM)
    w_kc = jnp.pad(kvb[..., :C_NOPE], no_pad + ((0, C_HEAD_PAD - C_NOPE),))
    w_kc = w_kc.reshape(depth, C_KV_RANK, C_HEADS * C_HEAD_PAD)
    w_vc = kvb[..., C_NOPE:].reshape(depth, C_KV_RANK, C_HEADS * C_VDIM)
    heads_per_tile = LANE // A_HEAD_DIM
    ones = _block_diag(jnp.full((heads_per_tile, A_HEAD_DIM, A_HEAD_DIM), 1.0 / A_HEAD_DIM, F32))
    ones = jnp.concatenate([ones, ones], axis=0).astype(BF16)
    return {
        'w_in': w_re, 'w_gate': w_in[:, :, OFF_GATE:].astype(BF16),
        'w_kc': w_kc.astype(BF16), 'w_vc': w_vc.astype(BF16), 'w_qb': qb.astype(BF16),
        'ones': jnp.broadcast_to(ones[None], (depth,) + ones.shape),
        'gk': jnp.tile(a_k_gain, (1, heads_per_tile))[:, None, :],
        'gq': jnp.tile(a_q_gain, (1, heads_per_tile))[:, None, :] * (A_HEAD_DIM ** -0.5 * LOG2_E),
        'gkv': c_kv_a_gain[:, None, :], 'gcq': c_q_a_gain[:, None, :],
    }


def kernel(x, c, ctx, c_ctx, w_mod, b_mod, w_in, a_q_gain, a_k_gain, c_q_a_gain, c_kv_a_gain, c_w_qb, c_w_kvb, s5_a_re, s5_a_im, s5_log_dt, s5_b_re, s5_b_im, s5_c_re, s5_c_im, s5_d, s5_w_glu, w_branch_a, w_branch_s5, w_branch_c, w_out, ln1_g, ln1_b, w_up, w_down, ln2_g, ln2_b):
    bsz, seq, _ = x.shape
    tabs_lat = _rope_lane_tables(seq, A_HEAD_DIM, 0) + _rope_lane_tables(seq, C_ROPE, C_NOPE)
    c_all = jnp.zeros((8, D_MODEL), F32).at[:bsz].set(c).at[bsz].set(c_ctx)
    zero_carry = jnp.zeros((bsz, 2, S5_LANES), F32)

    lw = _layer_weights(w_in, a_q_gain, a_k_gain, c_q_a_gain, c_kv_a_gain, c_w_qb, c_w_kvb)
    sw = _s5_weights(s5_a_re, s5_a_im, s5_log_dt, s5_b_re, s5_b_im, s5_c_re, s5_c_im)
    d_rows = s5_d[:, None, :]
    w_glu = s5_w_glu.astype(BF16)
    merge_w = [lw['w_gate'], w_branch_a.astype(BF16), w_branch_s5.astype(BF16), w_branch_c.astype(BF16),
               w_out.astype(BF16), ln1_g[:, None, :], ln1_b[:, None, :]]
    mlp_w = [w_up.astype(BF16), w_down.astype(BF16), ln2_g[:, None, :], ln2_b[:, None, :]]

    for l in range(DEPTH):
        last = l == DEPTH - 1
        mod = _modulation(c_all, w_mod, b_mod, l).reshape(8, N_MOD, D_MODEL)
        mod_lat, mod_ctx = mod[:bsz], mod[bsz:bsz + 1]

        ka_c, va_c, kc_c, vc_c, u_c, qa_c, qc_c = _inproj(ctx, mod_ctx, lw, l, tabs_lat, rope=False)
        ka, va, kc, vc, u, qa, qc = _inproj(x, mod_lat, lw, l, tabs_lat, rope=True)

        u_c, u = _interleave(u_c), _interleave(u)
        yf_c, carry_f = _s5_scan(u_c, zero_carry, sw, l, u_c, d_rows, w_glu, reverse=False, glu=False)
        yf, _ = _s5_scan(u, carry_f, sw, l, u, d_rows, w_glu, reverse=False, glu=False)
        ys_c, carry_b = _s5_scan(u_c, zero_carry, sw, l, yf_c, d_rows, w_glu, reverse=True, glu=True)
        ys, _ = _s5_scan(u, carry_b, sw, l, yf, d_rows, w_glu, reverse=True, glu=True)
        ys_c, ys = _deinterleave(ys_c), _deinterleave(ys)

        ya = _attention(qa, ka, ka_c, va, va_c, heads=A_HEADS, kv_heads=A_KV_HEADS,
                        dk=A_HEAD_DIM, dv=A_HEAD_DIM, name="attn_a")
        yc = _attention(qc, kc, kc_c, vc, vc_c, heads=C_HEADS, kv_heads=C_HEADS,
                        dk=C_HEAD_PAD, dv=C_VDIM, name="attn_c")
        x_mid = _token_call(_merge_kernel, "merge", x, mod_lat, [ya, ys, yc], merge_w, l)
        x_next = _token_call(_mlp_kernel, "mlp", x_mid, mod_lat, [], mlp_w, l)

        if not last:
            ya_c = _attention_ctx(qa_c, ka_c, va_c, heads=A_HEADS, kv_heads=A_KV_HEADS,
                                  dk=A_HEAD_DIM, dv=A_HEAD_DIM, name="attn_a_ctx")
            yc_c = _attention_ctx(qc_c, kc_c, vc_c, heads=C_HEADS, kv_heads=C_HEADS,
                                  dk=C_HEAD_PAD, dv=C_VDIM, name="attn_c_ctx")
            ctx_mid = _token_call(_merge_kernel, "merge_ctx", ctx, mod_ctx, [ya_c, ys_c, yc_c], merge_w, l)
            ctx = _token_call(_mlp_kernel, "mlp_ctx", ctx_mid, mod_ctx, [], mlp_w, l)
        x = x_next
    return x
```

```python
import functools
import math

import jax
import jax.numpy as jnp
from jax import lax
from jax.experimental import pallas as pl
from jax.experimental.pallas import tpu as pltpu

F32 = jnp.float32
BF16 = jnp.bfloat16

D_MODEL = 1024
DEPTH = 2
GRID_W = 64
ROPE_THETA = 10000.0
EPS = 1e-6

A_HEADS = 8
A_KV_HEADS = 2
A_HEAD_DIM = 64
A_WIDTH = A_HEADS * A_HEAD_DIM

S5_CH = 512
S5_GROUP_CH = 16
S5_GROUPS = S5_CH // S5_GROUP_CH
S5_STATE = 64
S5_LANES = S5_GROUPS * S5_STATE

C_HEADS = 8
C_NOPE = 64
C_ROPE = 32
C_VDIM = 64
C_Q_RANK = 768
C_KV_RANK = 256
C_QK_DIM = C_NOPE + C_ROPE
C_WIDTH = C_HEADS * C_VDIM

D_FF = 4 * D_MODEL
N_BRANCH = 3
N_MOD = 6
DEEPNORM_ALPHA = (2.0 * DEPTH) ** 0.25
LOG2_E = math.log2(math.e)

OFF_AK = 0
OFF_AV = OFF_AK + A_KV_HEADS * A_HEAD_DIM
OFF_CKV = OFF_AV + A_KV_HEADS * A_HEAD_DIM
OFF_CKR = OFF_CKV + C_KV_RANK
OFF_U = OFF_CKR + C_ROPE
OFF_AQ = OFF_U + S5_CH
OFF_CQ = OFF_AQ + A_WIDTH
OFF_GATE = OFF_CQ + C_Q_RANK
N_IN_COLS = OFF_GATE + N_BRANCH * D_MODEL

LANE = 128
C_HEAD_PAD = LANE

P_AK = 0
P_AV = P_AK + LANE
P_CKV = P_AV + LANE
P_CKR = P_CKV + C_KV_RANK
P_U = P_CKR + LANE
P_AQ = P_U + S5_CH
P_CQ = P_AQ + A_WIDTH
P_GATE = P_CQ + C_Q_RANK
P_COLS = P_GATE + N_BRANCH * D_MODEL

MLP_ROW_BLOCK = 512
MERGE_ROW_BLOCK = 1024
INPROJ_ROW_BLOCK = 1024
ATTN_TQ = 256
ATTN_KEY_CHUNK = 1024
S5_CHUNK = 256
S5_SUB = 8
S5_SUB_LEN = S5_CHUNK // S5_SUB
S5_LANE_CHUNK = 512
VMEM_LIMIT = 56 * 1024 * 1024


def _sel_spec(arr, idx):
    nd = arr.ndim - len(idx)
    return pl.BlockSpec((None,) * len(idx) + tuple(arr.shape[len(idx):]), lambda *_: tuple(idx) + (0,) * nd,
                        pipeline_mode=pl.Buffered(1))


def _params(n_grid):
    return pltpu.CompilerParams(dimension_semantics=("arbitrary",) * n_grid,
                                vmem_limit_bytes=VMEM_LIMIT)


def _layer_norm(x):
    mu = jnp.mean(x, axis=-1, keepdims=True)
    xc = x - mu
    var = jnp.mean(xc * xc, axis=-1, keepdims=True)
    return xc * lax.rsqrt(var + EPS)


def _dot(a, b):
    return jnp.dot(a, b, preferred_element_type=F32)


def _mod_kernel(c_ref, w_ref, b_ref, o_ref):
    s = jax.nn.silu(c_ref[...]).astype(BF16)
    o_ref[...] = _dot(s, w_ref[...].astype(BF16)) + b_ref[...]


def _modulation(c_all, w_mod, b_mod, layer):
    n = N_MOD * D_MODEL
    bn = n // 4
    return pl.pallas_call(
        _mod_kernel,
        grid=(n // bn,),
        in_specs=[pl.BlockSpec((8, D_MODEL), lambda j: (0, 0)),
                  pl.BlockSpec((None, D_MODEL, bn), lambda j: (layer, 0, j)),
                  pl.BlockSpec((None, 1, bn), lambda j: (layer, 0, j))],
        out_specs=pl.BlockSpec((8, bn), lambda j: (0, j)),
        out_shape=jax.ShapeDtypeStruct((8, n), F32),
        compiler_params=_params(1),
        name="mod",
    )(c_all, w_mod, b_mod.reshape(DEPTH, 1, n))


def _rope(x, cos, sin_next, sin_prev, shift):
    n = x.shape[-1]
    return x * cos + pltpu.roll(x, n - shift, 1) * sin_next + pltpu.roll(x, shift, 1) * sin_prev


def _head_rms_norm(p, ones_ref, gain):
    sq = p * p
    hi = sq.astype(BF16)
    lo = (sq - hi.astype(F32)).astype(BF16)
    ms = _dot(jnp.concatenate([hi, lo], axis=1), ones_ref[...])
    return p * lax.rsqrt(ms + EPS) * gain


def _row_rms_norm(p, gain):
    return p * lax.rsqrt(jnp.mean(p * p, axis=-1, keepdims=True) + EPS) * gain


def _value_tile(v, head):
    half = LANE // 2
    t = v[:, (head // 2) * LANE:(head // 2 + 1) * LANE]
    if head % 2:
        t = pltpu.roll(t, half, 1)
    low = lax.broadcasted_iota(jnp.int32, t.shape, 1) < half
    return jnp.where(low, t, 1.0).astype(BF16)


def _inproj_kernel(x_ref, mod_ref, w_ref, wkc_ref, wvc_ref, wqb_ref, ones_ref,
                   gk_ref, gq_ref, gkv_ref, gcq_ref,
                   cosa_ref, sna_ref, spa_ref, cosc_ref, snc_ref, spc_ref,
                   ka_ref, va_ref, kc_ref, vc_ref, u_ref, qa_ref, qc_ref, *, rope):
    mod = mod_ref[0]
    h = (_layer_norm(x_ref[0]) * (1.0 + mod[1:2]) + mod[0:1]).astype(BF16)

    def proj(a, b):
        return _dot(h, w_ref[:, a:b])

    def rope_a(t):
        return _rope(t, cosa_ref[...], sna_ref[...], spa_ref[...], A_HEAD_DIM // 4) if rope else t

    def rope_c(t):
        return _rope(t, cosc_ref[...], snc_ref[...], spc_ref[...], C_ROPE // 4) if rope else t

    state = proj(P_AK, P_U)
    kt = rope_a(_head_rms_norm(state[:, P_AK:P_AV], ones_ref, gk_ref[...])).T.astype(BF16)
    v = state[:, P_AV:P_CKV]
    for hh in range(A_KV_HEADS):
        ka_ref[0, hh] = kt[hh * A_HEAD_DIM:(hh + 1) * A_HEAD_DIM, :]
        va_ref[0, hh] = _value_tile(v, hh)

    ckv = _row_rms_norm(state[:, P_CKV:P_CKR], gkv_ref[...]).astype(BF16)
    k_rope = rope_c(state[:, P_CKR:P_U])
    k_nope = _dot(ckv, wkc_ref[...])
    vc = _dot(ckv, wvc_ref[...])
    for hh in range(C_HEADS):
        kc_ref[0, hh] = (k_nope[:, hh * C_HEAD_PAD:(hh + 1) * C_HEAD_PAD] + k_rope).T.astype(BF16)
        vc_ref[0, hh] = _value_tile(vc, hh)

    u_ref[0] = proj(P_U, P_AQ)

    aq = proj(P_AQ, P_CQ)
    for t in range(A_WIDTH // LANE):
        q = _head_rms_norm(aq[:, t * LANE:(t + 1) * LANE], ones_ref, gq_ref[...])
        qa_ref[0, :, t * LANE:(t + 1) * LANE] = rope_a(q).astype(BF16)

    cq = _row_rms_norm(proj(P_CQ, P_GATE), gcq_ref[...]).astype(BF16)
    qc = _dot(cq, wqb_ref[...])
    for hh in range(C_HEADS):
        q = rope_c(qc[:, hh * C_HEAD_PAD:(hh + 1) * C_HEAD_PAD])
        qc_ref[0, :, hh * C_HEAD_PAD:(hh + 1) * C_HEAD_PAD] = (q * (C_QK_DIM ** -0.5 * LOG2_E)).astype(BF16)


def _inproj(x, mod, lw, layer, tabs, rope):
    bsz, n_tok, _ = x.shape
    rb = min(INPROJ_ROW_BLOCK, n_tok)
    nb = n_tok // rb
    per_batch_mod = mod.shape[0] > 1

    def row_spec(width):
        return pl.BlockSpec((1, rb, width), lambda b, i: (b, i, 0))

    def head_spec(heads, width):
        return pl.BlockSpec((1, heads, rb, width), lambda b, i: (b, 0, i, 0))

    tab_spec = pl.BlockSpec((rb, LANE), lambda b, i: (i, 0))
    mod_spec = pl.BlockSpec((1, N_MOD, D_MODEL), (lambda b, i: (b, 0, 0)) if per_batch_mod else (lambda b, i: (0, 0, 0)))
    consts = [lw['w_in'], lw['w_kc'], lw['w_vc'], lw['w_qb'], lw['ones'],
              lw['gk'], lw['gq'], lw['gkv'], lw['gcq']]
    def head_t_spec(heads, width):
        return pl.BlockSpec((1, heads, width, rb), lambda b, i: (b, 0, 0, i))

    out_shape = [
        jax.ShapeDtypeStruct((bsz, A_KV_HEADS, A_HEAD_DIM, n_tok), BF16),
        jax.ShapeDtypeStruct((bsz, A_KV_HEADS, n_tok, 2 * A_HEAD_DIM), BF16),
        jax.ShapeDtypeStruct((bsz, C_HEADS, C_HEAD_PAD, n_tok), BF16),
        jax.ShapeDtypeStruct((bsz, C_HEADS, n_tok, 2 * C_VDIM), BF16),
        jax.ShapeDtypeStruct((bsz, n_tok, S5_CH), F32),
        jax.ShapeDtypeStruct((bsz, n_tok, A_WIDTH), BF16),
        jax.ShapeDtypeStruct((bsz, n_tok, C_HEADS * C_HEAD_PAD), BF16),
    ]
    out_specs = [head_t_spec(A_KV_HEADS, A_HEAD_DIM), head_spec(A_KV_HEADS, 2 * A_HEAD_DIM),
                 head_t_spec(C_HEADS, C_HEAD_PAD), head_spec(C_HEADS, 2 * C_VDIM),
                 row_spec(S5_CH), row_spec(A_WIDTH), row_spec(C_HEADS * C_HEAD_PAD)]
    return pl.pallas_call(
        functools.partial(_inproj_kernel, rope=rope),
        grid=(bsz, nb),
        in_specs=[row_spec(D_MODEL), mod_spec] + [_sel_spec(a, (layer,)) for a in consts] + [tab_spec] * 6,
        out_specs=out_specs,
        out_shape=out_shape,
        compiler_params=_params(2),
        name="inproj",
    )(x, mod, *consts, *tabs)


def _scores(q_ref, q_rows, kt_refs, s_ref, m_ref, *, shared_kv, dk):
    tq = ATTN_TQ
    q = q_ref[0, q_rows, :]
    if shared_kv:
        lhs = [(slice(0, 2 * tq), jnp.concatenate([q[:, :dk], q[:, dk:]], axis=0), 0)]
    else:
        lhs = [(slice(hh * tq, (hh + 1) * tq), q[:, hh * dk:(hh + 1) * dk], hh) for hh in range(2)]
    for rows, qh, hh in lhs:
        m = None
        off = 0
        for kt_ref in kt_refs:
            n = kt_ref.shape[-1]
            for c0 in range(0, n, ATTN_KEY_CHUNK):
                c1 = min(c0 + ATTN_KEY_CHUNK, n)
                s = _dot(qh, kt_ref[0, hh, :, c0:c1])
                s_ref[rows, off + c0:off + c1] = s
                mc = s.max(axis=-1, keepdims=True)
                m = mc if m is None else jnp.maximum(m, mc)
            off += n
        m_ref[rows] = m


def _exp_scores(s_ref, m_ref, p_ref):
    m = m_ref[...]
    n = s_ref.shape[-1]
    for c0 in range(0, n, ATTN_KEY_CHUNK):
        c1 = min(c0 + ATTN_KEY_CHUNK, n)
        p_ref[:, c0:c1] = jnp.exp2(s_ref[:, c0:c1] - m).astype(BF16)


def _weighted_values(p_ref, v_refs, o_ref, o_rows, *, shared_kv, dv):
    tq = ATTN_TQ
    for hh in range(2):
        rows = slice(hh * tq, (hh + 1) * tq)
        o = None
        off = 0
        for v_ref in v_refs:
            n = v_ref.shape[2]
            part = _dot(p_ref[rows, off:off + n], v_ref[0, 0 if shared_kv else hh])
            o = part if o is None else o + part
            off += n
        o_ref[0, o_rows, hh * dv:(hh + 1) * dv] = (o[:, :dv] / o[:, dv:]).astype(BF16)


def _attn_pipe_kernel(run_ref, q_ref, kt_lat, kt_ctx, v_lat, v_ctx, o_ref,
                      s_a, s_b, p_a, p_b, m_a, m_b, *, shared_kv, dk, dv):
    g = pl.program_id(0)

    @pl.when(g == 0)
    def _():
        s_b[...] = jnp.zeros_like(s_b)
        m_b[...] = jnp.zeros_like(m_b)
        p_a[...] = jnp.ones_like(p_a)

    kts, vs = [kt_lat, kt_ctx], [v_lat, v_ctx]

    def tick(rows, s_new, m_new, s_old, m_old, p_new, p_old):
        _scores(q_ref, rows, kts, s_new, m_new, shared_kv=shared_kv, dk=dk)
        _exp_scores(s_old, m_old, p_new)
        _weighted_values(p_old, vs, o_ref, rows, shared_kv=shared_kv, dv=dv)

    @pl.when(run_ref[0] == 1)
    def _():
        tick(slice(0, ATTN_TQ), s_a, m_a, s_b, m_b, p_b, p_a)

    @pl.when(run_ref[1] == 1)
    def _():
        tick(slice(ATTN_TQ, 2 * ATTN_TQ), s_b, m_b, s_a, m_a, p_a, p_b)


def _attn_ctx_kernel(q_ref, kt_ref, v_ref, o_ref, *, shared_kv, dk, dv):
    for hh in range(2):
        kv = 0 if shared_kv else hh
        s = _dot(q_ref[0, :, hh * dk:(hh + 1) * dk], kt_ref[0, kv])
        p = jnp.exp2(s - s.max(axis=-1, keepdims=True)).astype(BF16)
        o = _dot(p, v_ref[0, kv])
        o_ref[0, :, hh * dv:(hh + 1) * dv] = (o[:, :dv] / o[:, dv:]).astype(BF16)


def _attention(q, kt_lat, kt_ctx, v_lat, v_ctx, *, heads, kv_heads, dk, dv, name):
    bsz, lq, _ = q.shape
    shared_kv = kv_heads < heads
    group = heads // kv_heads
    n_pairs = heads // 2
    n_qb = lq // (2 * ATTN_TQ)
    n_steps = bsz * n_pairs * n_qb
    lag = 1
    kv_blk = 1 if shared_kv else 2
    n_lat, n_ctx = kt_lat.shape[-1], kt_ctx.shape[-1]

    def split(g):
        return g // (n_pairs * n_qb), (g // n_qb) % n_pairs, g % n_qb

    def q_map(g):
        b, p, i = split(jnp.minimum(g, n_steps - 1))
        return b, i, p

    def k_map(g):
        b, p, _ = split(jnp.minimum(g, n_steps - 1))
        return b, (2 * p) // group if shared_kv else p, 0, 0

    def v_map(g):
        b, p, _ = split(jnp.maximum(g - lag, 0))
        return b, (2 * p) // group if shared_kv else p, 0, 0

    def o_map(g):
        b, p, i = split(jnp.maximum(g - lag, 0))
        return b, i, p

    rows, keys = 2 * ATTN_TQ, n_lat + n_ctx
    return pl.pallas_call(
        functools.partial(_attn_pipe_kernel, shared_kv=shared_kv, dk=dk, dv=dv),
        grid=(n_steps + lag,),
        in_specs=[pl.BlockSpec(memory_space=pltpu.SMEM),
                  pl.BlockSpec((1, 2 * ATTN_TQ, 2 * dk), q_map),
                  pl.BlockSpec((1, kv_blk, dk, n_lat), k_map), pl.BlockSpec((1, kv_blk, dk, n_ctx), k_map),
                  pl.BlockSpec((1, kv_blk, n_lat, 2 * dv), v_map), pl.BlockSpec((1, kv_blk, n_ctx, 2 * dv), v_map)],
        out_specs=pl.BlockSpec((1, 2 * ATTN_TQ, 2 * dv), o_map),
        out_shape=jax.ShapeDtypeStruct((bsz, lq, heads * dv), BF16),
        scratch_shapes=[pltpu.VMEM((rows, keys), F32), pltpu.VMEM((rows, keys), F32),
                        pltpu.VMEM((rows, keys), BF16), pltpu.VMEM((rows, keys), BF16)]
                       + [pltpu.VMEM((rows, 1), F32)] * 2,
        compiler_params=_params(1),
        name=name,
    )(jnp.ones((2,), jnp.int32), q, kt_lat, kt_ctx, v_lat, v_ctx)


def _attention_ctx(q, kt, v, *, heads, kv_heads, dk, dv, name):
    bsz, lq, _ = q.shape
    shared_kv = kv_heads < heads
    group = heads // kv_heads
    kv_blk = 1 if shared_kv else 2
    kmap = (lambda b, p: (b, (2 * p) // group, 0, 0)) if shared_kv else (lambda b, p: (b, p, 0, 0))
    return pl.pallas_call(
        functools.partial(_attn_ctx_kernel, shared_kv=shared_kv, dk=dk, dv=dv),
        grid=(bsz, heads // 2),
        in_specs=[pl.BlockSpec((1, lq, 2 * dk), lambda b, p: (b, 0, p)),
                  pl.BlockSpec((1, kv_blk, dk, kt.shape[-1]), kmap),
                  pl.BlockSpec((1, kv_blk, v.shape[2], 2 * dv), kmap)],
        out_specs=pl.BlockSpec((1, lq, 2 * dv), lambda b, p: (b, 0, p)),
        out_shape=jax.ShapeDtypeStruct((bsz, lq, heads * dv), BF16),
        compiler_params=_params(2),
        name=name,
    )(q, kt, v)


def _s5_kernel(u_ref, cin_ref, wb_ref, a_ref, apow_ref, wc_ref, yprev_ref, d_ref, wglu_ref,
               out_ref, cout_ref, bu_r, bu_i, xb, car, *, reverse, glu):
    @pl.when(pl.program_id(1) == 0)
    def _():
        car[...] = cin_ref[0]

    ub = u_ref[0].astype(BF16)
    n_tiles = S5_CH // LANE
    tile_states = S5_LANES // n_tiles
    for t in range(n_tiles):
        r = _dot(ub[:, t * LANE:(t + 1) * LANE], wb_ref[t])
        bu_r[:, t * tile_states:(t + 1) * tile_states] = r[:, :tile_states]
        bu_i[:, t * tile_states:(t + 1) * tile_states] = r[:, tile_states:]

    steps = range(S5_SUB_LEN - 1, -1, -1) if reverse else range(S5_SUB_LEN)
    subs = range(S5_SUB - 1, -1, -1) if reverse else range(S5_SUB)
    pack_rows = 2 * S5_SUB
    for c in range(S5_LANES // S5_LANE_CHUNK):
        ls = slice(c * S5_LANE_CHUNK, (c + 1) * S5_LANE_CHUNK)
        ar = jnp.broadcast_to(a_ref[0:1, ls], (S5_SUB, S5_LANE_CHUNK))
        ai = jnp.broadcast_to(a_ref[1:2, ls], (S5_SUB, S5_LANE_CHUNK))
        xr = jnp.zeros((S5_SUB, S5_LANE_CHUNK), F32)
        xi = jnp.zeros((S5_SUB, S5_LANE_CHUNK), F32)
        for i in steps:
            rows = slice(i * S5_SUB, (i + 1) * S5_SUB)
            xr, xi = (ar * xr - ai * xi + bu_r[rows, ls], ar * xi + ai * xr + bu_i[rows, ls])
            bu_r[rows, ls] = xr
            bu_i[rows, ls] = xi
        cr, ci = car[0:1, ls], car[1:2, ls]
        a_sub_r, a_sub_i = a_ref[2:3, ls], a_ref[3:4, ls]
        crs, cis = [None] * S5_SUB, [None] * S5_SUB
        for j in subs:
            crs[j], cis[j] = cr, ci
            cr, ci = (a_sub_r * cr - a_sub_i * ci + xr[j:j + 1], a_sub_r * ci + a_sub_i * cr + xi[j:j + 1])
        car[0:1, ls] = cr
        car[1:2, ls] = ci
        cmr = jnp.concatenate(crs * (pack_rows // S5_SUB), axis=0)
        cmi = jnp.concatenate(cis * (pack_rows // S5_SUB), axis=0)
        col = (c * S5_LANE_CHUNK // tile_states) * 2 * tile_states + (c * S5_LANE_CHUNK) % tile_states
        for g in range(S5_CHUNK // pack_rows):
            rows = slice(g * pack_rows, (g + 1) * pack_rows)
            pr, pi = apow_ref[0, rows, ls], apow_ref[1, rows, ls]
            xb[rows, col:col + S5_LANE_CHUNK] = (bu_r[rows, ls] + (pr * cmr - pi * cmi)).astype(BF16)
            xb[rows, col + tile_states:col + tile_states + S5_LANE_CHUNK] = (
                bu_i[rows, ls] + (pr * cmi + pi * cmr)).astype(BF16)
    cout_ref[0] = car[...]

    ys = [_dot(xb[:, t * 2 * tile_states:(t + 1) * 2 * tile_states], wc_ref[t]) for t in range(n_tiles)]
    y = jnp.concatenate(ys, axis=-1)
    if glu:
        y = y + yprev_ref[0] + d_ref[...] * u_ref[0]
        hg = _dot(jax.nn.gelu(y).astype(BF16), wglu_ref[...])
        out_ref[0] = (hg[:, :S5_CH] * jax.nn.sigmoid(hg[:, S5_CH:])).astype(out_ref.dtype)
    else:
        out_ref[0] = y


def _s5_scan(u, carry_in, sw, layer, yprev, d, w_glu, *, reverse, glu):
    bsz, n_tok, _ = u.shape
    nc = n_tok // S5_CHUNK
    order = (lambda b, i: (b, nc - 1 - i, 0)) if reverse else (lambda b, i: (b, i, 0))
    tile_states = S5_LANES // (S5_CH // LANE)
    assert tile_states % S5_LANE_CHUNK == 0
    row_spec = pl.BlockSpec((1, S5_CHUNK, S5_CH), order)
    carry_spec = pl.BlockSpec((1, 2, S5_LANES), lambda b, i: (b, 0, 0))
    consts = [sw['wb'], sw['a'], sw['apow'], sw['wc']]
    out, carry = pl.pallas_call(
        functools.partial(_s5_kernel, reverse=reverse, glu=glu),
        grid=(bsz, nc),
        in_specs=[row_spec, carry_spec] + [_sel_spec(a, (layer, int(reverse))) for a in consts]
                 + [row_spec, _sel_spec(d, (layer,)), _sel_spec(w_glu, (layer,))],
        out_specs=[row_spec, carry_spec],
        out_shape=[jax.ShapeDtypeStruct((bsz, n_tok, S5_CH), BF16 if glu else F32),
                   jax.ShapeDtypeStruct((bsz, 2, S5_LANES), F32)],
        scratch_shapes=[pltpu.VMEM((S5_CHUNK, S5_LANES), F32), pltpu.VMEM((S5_CHUNK, S5_LANES), F32),
                        pltpu.VMEM((S5_CHUNK, 2 * S5_LANES), BF16), pltpu.VMEM((2, S5_LANES), F32)],
        compiler_params=_params(2),
        name="s5_bwd" if reverse else "s5_fwd",
    )(u, carry_in, *consts, yprev, d, w_glu)
    return out, carry


def _merge_kernel(x_ref, mod_ref, ya_ref, ys_ref, yc_ref, wg_ref, wa_ref, ws_ref, wc_ref, wo_ref,
                  g_ref, b_ref, o_ref):
    x = x_ref[0]
    mod = mod_ref[0]
    h = (_layer_norm(x) * (1.0 + mod[1:2]) + mod[0:1]).astype(BF16)
    merged = None
    for t, (y_ref, w_ref) in enumerate(((ya_ref, wa_ref), (ys_ref, ws_ref), (yc_ref, wc_ref))):
        gate = jax.nn.sigmoid(_dot(h, wg_ref[:, t * D_MODEL:(t + 1) * D_MODEL]))
        term = gate * _dot(y_ref[0], w_ref[...])
        merged = term if merged is None else merged + term
    mix = _dot(merged.astype(BF16), wo_ref[...])
    y = DEEPNORM_ALPHA * x + mod[2:3] * mix
    o_ref[0] = _layer_norm(y) * g_ref[...] + b_ref[...]


def _mlp_kernel(x_ref, mod_ref, wu_ref, wd_ref, g_ref, b_ref, o_ref):
    x = x_ref[0]
    mod = mod_ref[0]
    h = (_layer_norm(x) * (1.0 + mod[4:5]) + mod[3:4]).astype(BF16)
    up = jnp.square(jnp.maximum(_dot(h, wu_ref[...]), 0.0)).astype(BF16)
    y = DEEPNORM_ALPHA * x + mod[5:6] * _dot(up, wd_ref[...])
    o_ref[0] = _layer_norm(y) * g_ref[...] + b_ref[...]


def _token_call(kernel, name, x, mod, rows, consts, layer, row_block):
    bsz, n_tok, _ = x.shape
    per_batch_mod = mod.shape[0] > 1
    rb = min(row_block, n_tok)

    def row_spec(width):
        return pl.BlockSpec((1, rb, width), lambda b, i: (b, i, 0))

    mod_spec = pl.BlockSpec((1, N_MOD, D_MODEL), (lambda b, i: (b, 0, 0)) if per_batch_mod else (lambda b, i: (0, 0, 0)))
    return pl.pallas_call(
        kernel,
        grid=(bsz, n_tok // rb),
        in_specs=[row_spec(D_MODEL), mod_spec] + [row_spec(r.shape[-1]) for r in rows]
                 + [_sel_spec(a, (layer,)) for a in consts],
        out_specs=row_spec(D_MODEL),
        out_shape=jax.ShapeDtypeStruct((bsz, n_tok, D_MODEL), F32),
        compiler_params=_params(2),
        name=name,
    )(x, mod, *rows, *consts)


def _axial_rope_tables(rows, dim):
    half = dim // 2
    inv = ROPE_THETA ** (-jnp.arange(0, half, 2, dtype=F32) / half)
    row = jnp.repeat(jnp.arange(rows, dtype=F32), GRID_W)
    col = jnp.tile(jnp.arange(GRID_W, dtype=F32), rows)
    ang_r = row[:, None] * inv
    ang_c = col[:, None] * inv
    ang = jnp.concatenate([ang_r, ang_r, ang_c, ang_c], axis=-1)
    return jnp.cos(ang), jnp.sin(ang)


def _rope_lane_tables(n_tok, dim, lane_off):
    cos, sin = _axial_rope_tables(n_tok // GRID_W, dim)
    reps = (LANE - lane_off) // dim if lane_off == 0 else 1
    cos_t = jnp.ones((n_tok, LANE), F32).at[:, lane_off:lane_off + reps * dim].set(jnp.tile(cos, (1, reps)))
    sin_t = jnp.zeros((n_tok, LANE), F32).at[:, lane_off:lane_off + reps * dim].set(jnp.tile(sin, (1, reps)))
    quarter = dim // 4
    first = (jnp.arange(LANE) % (2 * quarter)) < quarter
    return cos_t, jnp.where(first, -sin_t, 0.0), jnp.where(first, 0.0, sin_t)


def _block_diag(blocks):
    n, r, c = blocks.shape
    eye = jnp.eye(n, dtype=blocks.dtype)
    return (eye[:, None, :, None] * blocks[:, :, None, :]).reshape(n * r, n * c)


def _s5_discretize(a_re, a_im, log_dt, b_re, b_im):
    dt = jnp.exp(log_dt)[:, None]
    mag = jnp.exp(a_re * dt)
    abar_r = mag * jnp.cos(a_im * dt)
    abar_i = mag * jnp.sin(a_im * dt)
    den = a_re * a_re + a_im * a_im
    nr = abar_r - 1.0
    coef_r = (nr * a_re + abar_i * a_im) / den
    coef_i = (abar_i * a_re - nr * a_im) / den
    bbar_r = coef_r[..., None] * b_re - coef_i[..., None] * b_im
    bbar_i = coef_r[..., None] * b_im + coef_i[..., None] * b_re
    return abar_r, abar_i, bbar_r, bbar_i


def _s5_direction_weights(a_re, a_im, log_dt, b_re, b_im, c_re, c_im):
    abar_r, abar_i, bbar_r, bbar_i = _s5_discretize(a_re, a_im, log_dt, b_re, b_im)
    groups_per_tile = LANE // S5_GROUP_CH
    n_tiles = S5_GROUPS // groups_per_tile

    def in_tile(bbar):
        blk = jnp.swapaxes(bbar, 1, 2).reshape(n_tiles, groups_per_tile, S5_GROUP_CH, S5_STATE)
        return jax.vmap(_block_diag)(blk)

    def out_tile(cm):
        blk = jnp.swapaxes(cm, 1, 2).reshape(n_tiles, groups_per_tile, S5_STATE, S5_GROUP_CH)
        return jax.vmap(_block_diag)(blk)

    wb = jnp.concatenate([in_tile(bbar_r), in_tile(bbar_i)], axis=-1).astype(BF16)
    wc = jnp.concatenate([out_tile(c_re), -out_tile(c_im)], axis=1).astype(BF16)

    def step(carry, _):
        pr, pi = carry
        nxt = (pr * abar_r - pi * abar_i, pr * abar_i + pi * abar_r)
        return nxt, nxt
    _, (pows_r, pows_i) = lax.scan(step, (jnp.ones_like(abar_r), jnp.zeros_like(abar_r)), None, length=S5_SUB_LEN)
    pows = jnp.stack([pows_r.reshape(S5_SUB_LEN, S5_LANES), pows_i.reshape(S5_SUB_LEN, S5_LANES)])
    return wb, wc, jnp.stack([abar_r.reshape(-1), abar_i.reshape(-1)]), pows


def _s5_weights(a_re, a_im, log_dt, b_re, b_im, c_re, c_im):
    wb, wc, abar, pows = jax.vmap(jax.vmap(_s5_direction_weights))(a_re, a_im, log_dt, b_re, b_im, c_re, c_im)
    sub = pows[:, :, :, -1]
    pows = jnp.stack([pows[:, 0], pows[:, 1, :, ::-1]], axis=1)
    a = jnp.concatenate([abar, sub], axis=2)
    apow = jnp.repeat(pows, S5_SUB, axis=3)
    return {'wb': wb, 'a': a, 'apow': apow, 'wc': wc}


def _interleave(t):
    b, n, w = t.shape
    return t.reshape(b, n // S5_CHUNK, S5_SUB, S5_SUB_LEN, w).swapaxes(2, 3).reshape(b, n, w)


def _deinterleave(t):
    b, n, w = t.shape
    return t.reshape(b, n // S5_CHUNK, S5_SUB_LEN, S5_SUB, w).swapaxes(2, 3).reshape(b, n, w)


def _layer_weights(w_in, a_q_gain, a_k_gain, c_q_a_gain, c_kv_a_gain, c_w_qb, c_w_kvb):
    depth = w_in.shape[0]
    ckr = jnp.zeros((depth, D_MODEL, LANE), F32).at[:, :, C_NOPE:C_QK_DIM].set(w_in[:, :, OFF_CKR:OFF_U])
    w_re = jnp.concatenate([w_in[:, :, OFF_AK:OFF_CKR], ckr, w_in[:, :, OFF_U:OFF_GATE]], axis=2).astype(BF16)
    assert w_re.shape[2] == P_GATE
    no_pad = ((0, 0), (0, 0), (0, 0))
    qb = c_w_qb.reshape(depth, C_Q_RANK, C_HEADS, C_QK_DIM)
    qb = jnp.pad(qb, no_pad + ((0, C_HEAD_PAD - C_QK_DIM),)).reshape(depth, C_Q_RANK, C_HEADS * C_HEAD_PAD)
    kvb = c_w_kvb.reshape(depth, C_KV_RANK, C_HEADS, C_NOPE + C_VDIM)
    w_kc = jnp.pad(kvb[..., :C_NOPE], no_pad + ((0, C_HEAD_PAD - C_NOPE),))
    w_kc = w_kc.reshape(depth, C_KV_RANK, C_HEADS * C_HEAD_PAD)
    w_vc = kvb[..., C_NOPE:].reshape(depth, C_KV_RANK, C_HEADS * C_VDIM)
    heads_per_tile = LANE // A_HEAD_DIM
    ones = _block_diag(jnp.full((heads_per_tile, A_HEAD_DIM, A_HEAD_DIM), 1.0 / A_HEAD_DIM, F32))
    ones = jnp.concatenate([ones, ones], axis=0).astype(BF16)
    return {
        'w_in': w_re, 'w_gate': w_in[:, :, OFF_GATE:].astype(BF16),
        'w_kc': w_kc.astype(BF16), 'w_vc': w_vc.astype(BF16), 'w_qb': qb.astype(BF16),
        'ones': jnp.broadcast_to(ones[None], (depth,) + ones.shape),
        'gk': jnp.tile(a_k_gain, (1, heads_per_tile))[:, None, :],
        'gq': jnp.tile(a_q_gain, (1, heads_per_tile))[:, None, :] * (A_HEAD_DIM ** -0.5 * LOG2_E),
        'gkv': c_kv_a_gain[:, None, :], 'gcq': c_q_a_gain[:, None, :],
    }


def kernel(x, c, ctx, c_ctx, w_mod, b_mod, w_in, a_q_gain, a_k_gain, c_q_a_gain, c_kv_a_gain, c_w_qb, c_w_kvb, s5_a_re, s5_a_im, s5_log_dt, s5_b_re, s5_b_im, s5_c_re, s5_c_im, s5_d, s5_w_glu, w_branch_a, w_branch_s5, w_branch_c, w_out, ln1_g, ln1_b, w_up, w_down, ln2_g, ln2_b):
    bsz, seq, _ = x.shape
    tabs_lat = _rope_lane_tables(seq, A_HEAD_DIM, 0) + _rope_lane_tables(seq, C_ROPE, C_NOPE)
    c_all = jnp.zeros((8, D_MODEL), F32).at[:bsz].set(c).at[bsz].set(c_ctx)
    zero_carry = jnp.zeros((bsz, 2, S5_LANES), F32)

    lw = _layer_weights(w_in, a_q_gain, a_k_gain, c_q_a_gain, c_kv_a_gain, c_w_qb, c_w_kvb)
    sw = _s5_weights(s5_a_re, s5_a_im, s5_log_dt, s5_b_re, s5_b_im, s5_c_re, s5_c_im)
    d_rows = s5_d[:, None, :]
    w_glu = s5_w_glu.astype(BF16)
    merge_w = [lw['w_gate'], w_branch_a.astype(BF16), w_branch_s5.astype(BF16), w_branch_c.astype(BF16),
               w_out.astype(BF16), ln1_g[:, None, :], ln1_b[:, None, :]]
    mlp_w = [w_up.astype(BF16), w_down.astype(BF16), ln2_g[:, None, :], ln2_b[:, None, :]]

    for l in range(DEPTH):
        last = l == DEPTH - 1
        mod = _modulation(c_all, w_mod, b_mod, l).reshape(8, N_MOD, D_MODEL)
        mod_lat, mod_ctx = mod[:bsz], mod[bsz:bsz + 1]

        ka_c, va_c, kc_c, vc_c, u_c, qa_c, qc_c = _inproj(ctx, mod_ctx, lw, l, tabs_lat, rope=False)
        ka, va, kc, vc, u, qa, qc = _inproj(x, mod_lat, lw, l, tabs_lat, rope=True)

        u_c, u = _interleave(u_c), _interleave(u)
        yf_c, carry_f = _s5_scan(u_c, zero_carry, sw, l, u_c, d_rows, w_glu, reverse=False, glu=False)
        yf, _ = _s5_scan(u, carry_f, sw, l, u, d_rows, w_glu, reverse=False, glu=False)
        ys_c, carry_b = _s5_scan(u_c, zero_carry, sw, l, yf_c, d_rows, w_glu, reverse=True, glu=True)
        ys, _ = _s5_scan(u, carry_b, sw, l, yf, d_rows, w_glu, reverse=True, glu=True)
        ys_c, ys = _deinterleave(ys_c), _deinterleave(ys)

        ya = _attention(qa, ka, ka_c, va, va_c, heads=A_HEADS, kv_heads=A_KV_HEADS,
                        dk=A_HEAD_DIM, dv=A_HEAD_DIM, name="attn_a")
        yc = _attention(qc, kc, kc_c, vc, vc_c, heads=C_HEADS, kv_heads=C_HEADS,
                        dk=C_HEAD_PAD, dv=C_VDIM, name="attn_c")
        x_mid = _token_call(_merge_kernel, "merge", x, mod_lat, [ya, ys, yc], merge_w, l, MERGE_ROW_BLOCK)
        x_next = _token_call(_mlp_kernel, "mlp", x_mid, mod_lat, [], mlp_w, l, MLP_ROW_BLOCK)

        if not last:
            ya_c = _attention_ctx(qa_c, ka_c, va_c, heads=A_HEADS, kv_heads=A_KV_HEADS,
                                  dk=A_HEAD_DIM, dv=A_HEAD_DIM, name="attn_a_ctx")
            yc_c = _attention_ctx(qc_c, kc_c, vc_c, heads=C_HEADS, kv_heads=C_HEADS,
                                  dk=C_HEAD_PAD, dv=C_VDIM, name="attn_c_ctx")
            ctx_mid = _token_call(_merge_kernel, "merge_ctx", ctx, mod_ctx, [ya_c, ys_c, yc_c], merge_w, l,
                                  MERGE_ROW_BLOCK)
            ctx = _token_call(_mlp_kernel, "mlp_ctx", ctx_mid, mod_ctx, [], mlp_w, l, MLP_ROW_BLOCK)
        x = x_next
    return x
```

```python
import functools
import math

import jax
import jax.numpy as jnp
from jax import lax
from jax.experimental import pallas as pl
from jax.experimental.pallas import tpu as pltpu

F32 = jnp.float32
BF16 = jnp.bfloat16

D_MODEL = 1024
DEPTH = 2
GRID_W = 64
ROPE_THETA = 10000.0
EPS = 1e-6

A_HEADS = 8
A_KV_HEADS = 2
A_HEAD_DIM = 64
A_WIDTH = A_HEADS * A_HEAD_DIM

S5_CH = 512
S5_GROUP_CH = 16
S5_GROUPS = S5_CH // S5_GROUP_CH
S5_STATE = 64
S5_LANES = S5_GROUPS * S5_STATE

C_HEADS = 8
C_NOPE = 64
C_ROPE = 32
C_VDIM = 64
C_Q_RANK = 768
C_KV_RANK = 256
C_QK_DIM = C_NOPE + C_ROPE
C_WIDTH = C_HEADS * C_VDIM

D_FF = 4 * D_MODEL
N_BRANCH = 3
N_MOD = 6
DEEPNORM_ALPHA = (2.0 * DEPTH) ** 0.25
LOG2_E = math.log2(math.e)

OFF_AK = 0
OFF_AV = OFF_AK + A_KV_HEADS * A_HEAD_DIM
OFF_CKV = OFF_AV + A_KV_HEADS * A_HEAD_DIM
OFF_CKR = OFF_CKV + C_KV_RANK
OFF_U = OFF_CKR + C_ROPE
OFF_AQ = OFF_U + S5_CH
OFF_CQ = OFF_AQ + A_WIDTH
OFF_GATE = OFF_CQ + C_Q_RANK
N_IN_COLS = OFF_GATE + N_BRANCH * D_MODEL

LANE = 128
C_HEAD_PAD = LANE

P_AK = 0
P_AV = P_AK + LANE
P_CKV = P_AV + LANE
P_CKR = P_CKV + C_KV_RANK
P_U = P_CKR + LANE
P_AQ = P_U + S5_CH
P_CQ = P_AQ + A_WIDTH
P_GATE = P_CQ + C_Q_RANK
P_COLS = P_GATE + N_BRANCH * D_MODEL

MLP_ROW_BLOCK = 512
MERGE_ROW_BLOCK = 1024
INPROJ_ROW_BLOCK = 1024
ATTN_TQ = 256
ATTN_KEY_CHUNK = 1024
S5_CHUNK = 256
S5_CHUNKS_PER_STEP = 2
S5_SUB = 8
S5_SUB_LEN = S5_CHUNK // S5_SUB
S5_LANE_CHUNK = 512
VMEM_LIMIT = 56 * 1024 * 1024


def _sel_spec(arr, idx):
    nd = arr.ndim - len(idx)
    return pl.BlockSpec((None,) * len(idx) + tuple(arr.shape[len(idx):]), lambda *_: tuple(idx) + (0,) * nd,
                        pipeline_mode=pl.Buffered(1))


def _params(n_grid):
    return pltpu.CompilerParams(dimension_semantics=("arbitrary",) * n_grid,
                                vmem_limit_bytes=VMEM_LIMIT)


def _layer_norm(x):
    mu = jnp.mean(x, axis=-1, keepdims=True)
    xc = x - mu
    var = jnp.mean(xc * xc, axis=-1, keepdims=True)
    return xc * lax.rsqrt(var + EPS)


def _dot(a, b):
    return jnp.dot(a, b, preferred_element_type=F32)


def _mod_kernel(c_ref, w_ref, b_ref, o_ref):
    s = jax.nn.silu(c_ref[...]).astype(BF16)
    o_ref[...] = _dot(s, w_ref[...].astype(BF16)) + b_ref[...]


def _modulation(c_all, w_mod, b_mod, layer):
    n = N_MOD * D_MODEL
    bn = n // 4
    return pl.pallas_call(
        _mod_kernel,
        grid=(n // bn,),
        in_specs=[pl.BlockSpec((8, D_MODEL), lambda j: (0, 0)),
                  pl.BlockSpec((None, D_MODEL, bn), lambda j: (layer, 0, j)),
                  pl.BlockSpec((None, 1, bn), lambda j: (layer, 0, j))],
        out_specs=pl.BlockSpec((8, bn), lambda j: (0, j)),
        out_shape=jax.ShapeDtypeStruct((8, n), F32),
        compiler_params=_params(1),
        name="mod",
    )(c_all, w_mod, b_mod.reshape(DEPTH, 1, n))


def _rope(x, cos, sin_next, sin_prev, shift):
    n = x.shape[-1]
    return x * cos + pltpu.roll(x, n - shift, 1) * sin_next + pltpu.roll(x, shift, 1) * sin_prev


def _head_rms_norm(p, ones_ref, gain):
    sq = p * p
    hi = sq.astype(BF16)
    lo = (sq - hi.astype(F32)).astype(BF16)
    ms = _dot(jnp.concatenate([hi, lo], axis=1), ones_ref[...])
    return p * lax.rsqrt(ms + EPS) * gain


def _row_rms_norm(p, gain):
    return p * lax.rsqrt(jnp.mean(p * p, axis=-1, keepdims=True) + EPS) * gain


def _value_tile(v, head):
    half = LANE // 2
    t = v[:, (head // 2) * LANE:(head // 2 + 1) * LANE]
    if head % 2:
        t = pltpu.roll(t, half, 1)
    low = lax.broadcasted_iota(jnp.int32, t.shape, 1) < half
    return jnp.where(low, t, 1.0).astype(BF16)


def _inproj_kernel(x_ref, mod_ref, w_ref, wkc_ref, wvc_ref, wqb_ref, ones_ref,
                   gk_ref, gq_ref, gkv_ref, gcq_ref,
                   cosa_ref, sna_ref, spa_ref, cosc_ref, snc_ref, spc_ref,
                   ka_ref, va_ref, kc_ref, vc_ref, u_ref, qa_ref, qc_ref, *, rope):
    mod = mod_ref[0]
    h = (_layer_norm(x_ref[0]) * (1.0 + mod[1:2]) + mod[0:1]).astype(BF16)

    def proj(a, b):
        return _dot(h, w_ref[:, a:b])

    def rope_a(t):
        return _rope(t, cosa_ref[...], sna_ref[...], spa_ref[...], A_HEAD_DIM // 4) if rope else t

    def rope_c(t):
        return _rope(t, cosc_ref[...], snc_ref[...], spc_ref[...], C_ROPE // 4) if rope else t

    state = proj(P_AK, P_U)
    kt = rope_a(_head_rms_norm(state[:, P_AK:P_AV], ones_ref, gk_ref[...])).T.astype(BF16)
    v = state[:, P_AV:P_CKV]
    for hh in range(A_KV_HEADS):
        ka_ref[0, hh] = kt[hh * A_HEAD_DIM:(hh + 1) * A_HEAD_DIM, :]
        va_ref[0, hh] = _value_tile(v, hh)

    ckv = _row_rms_norm(state[:, P_CKV:P_CKR], gkv_ref[...]).astype(BF16)
    k_rope = rope_c(state[:, P_CKR:P_U])
    k_nope = _dot(ckv, wkc_ref[...])
    vc = _dot(ckv, wvc_ref[...])
    for hh in range(C_HEADS):
        kc_ref[0, hh] = (k_nope[:, hh * C_HEAD_PAD:(hh + 1) * C_HEAD_PAD] + k_rope).T.astype(BF16)
        vc_ref[0, hh] = _value_tile(vc, hh)

    u_ref[0] = proj(P_U, P_AQ)

    aq = proj(P_AQ, P_CQ)
    for t in range(A_WIDTH // LANE):
        q = _head_rms_norm(aq[:, t * LANE:(t + 1) * LANE], ones_ref, gq_ref[...])
        qa_ref[0, :, t * LANE:(t + 1) * LANE] = rope_a(q).astype(BF16)

    cq = _row_rms_norm(proj(P_CQ, P_GATE), gcq_ref[...]).astype(BF16)
    qc = _dot(cq, wqb_ref[...])
    for hh in range(C_HEADS):
        q = rope_c(qc[:, hh * C_HEAD_PAD:(hh + 1) * C_HEAD_PAD])
        qc_ref[0, :, hh * C_HEAD_PAD:(hh + 1) * C_HEAD_PAD] = (q * (C_QK_DIM ** -0.5 * LOG2_E)).astype(BF16)


def _inproj(x, mod, lw, layer, tabs, rope):
    bsz, n_tok, _ = x.shape
    rb = min(INPROJ_ROW_BLOCK, n_tok)
    nb = n_tok // rb
    per_batch_mod = mod.shape[0] > 1

    def row_spec(width):
        return pl.BlockSpec((1, rb, width), lambda b, i: (b, i, 0))

    def head_spec(heads, width):
        return pl.BlockSpec((1, heads, rb, width), lambda b, i: (b, 0, i, 0))

    tab_spec = pl.BlockSpec((rb, LANE), lambda b, i: (i, 0))
    mod_spec = pl.BlockSpec((1, N_MOD, D_MODEL), (lambda b, i: (b, 0, 0)) if per_batch_mod else (lambda b, i: (0, 0, 0)))
    consts = [lw['w_in'], lw['w_kc'], lw['w_vc'], lw['w_qb'], lw['ones'],
              lw['gk'], lw['gq'], lw['gkv'], lw['gcq']]
    def head_t_spec(heads, width):
        return pl.BlockSpec((1, heads, width, rb), lambda b, i: (b, 0, 0, i))

    out_shape = [
        jax.ShapeDtypeStruct((bsz, A_KV_HEADS, A_HEAD_DIM, n_tok), BF16),
        jax.ShapeDtypeStruct((bsz, A_KV_HEADS, n_tok, 2 * A_HEAD_DIM), BF16),
        jax.ShapeDtypeStruct((bsz, C_HEADS, C_HEAD_PAD, n_tok), BF16),
        jax.ShapeDtypeStruct((bsz, C_HEADS, n_tok, 2 * C_VDIM), BF16),
        jax.ShapeDtypeStruct((bsz, n_tok, S5_CH), F32),
        jax.ShapeDtypeStruct((bsz, n_tok, A_WIDTH), BF16),
        jax.ShapeDtypeStruct((bsz, n_tok, C_HEADS * C_HEAD_PAD), BF16),
    ]
    out_specs = [head_t_spec(A_KV_HEADS, A_HEAD_DIM), head_spec(A_KV_HEADS, 2 * A_HEAD_DIM),
                 head_t_spec(C_HEADS, C_HEAD_PAD), head_spec(C_HEADS, 2 * C_VDIM),
                 row_spec(S5_CH), row_spec(A_WIDTH), row_spec(C_HEADS * C_HEAD_PAD)]
    return pl.pallas_call(
        functools.partial(_inproj_kernel, rope=rope),
        grid=(bsz, nb),
        in_specs=[row_spec(D_MODEL), mod_spec] + [_sel_spec(a, (layer,)) for a in consts] + [tab_spec] * 6,
        out_specs=out_specs,
        out_shape=out_shape,
        compiler_params=_params(2),
        name="inproj",
    )(x, mod, *consts, *tabs)


def _scores(q_ref, q_rows, kt_refs, s_ref, m_ref, *, shared_kv, dk):
    tq = ATTN_TQ
    q = q_ref[0, q_rows, :]
    if shared_kv:
        lhs = [(slice(0, 2 * tq), jnp.concatenate([q[:, :dk], q[:, dk:]], axis=0), 0)]
    else:
        lhs = [(slice(hh * tq, (hh + 1) * tq), q[:, hh * dk:(hh + 1) * dk], hh) for hh in range(2)]
    for rows, qh, hh in lhs:
        m = None
        off = 0
        for kt_ref in kt_refs:
            n = kt_ref.shape[-1]
            for c0 in range(0, n, ATTN_KEY_CHUNK):
                c1 = min(c0 + ATTN_KEY_CHUNK, n)
                s = _dot(qh, kt_ref[0, hh, :, c0:c1])
                s_ref[rows, off + c0:off + c1] = s
                mc = s.max(axis=-1, keepdims=True)
                m = mc if m is None else jnp.maximum(m, mc)
            off += n
        m_ref[rows] = m


def _exp_scores(s_ref, m_ref, p_ref):
    m = m_ref[...]
    n = s_ref.shape[-1]
    for c0 in range(0, n, ATTN_KEY_CHUNK):
        c1 = min(c0 + ATTN_KEY_CHUNK, n)
        p_ref[:, c0:c1] = jnp.exp2(s_ref[:, c0:c1] - m).astype(BF16)


def _weighted_values(p_ref, v_refs, o_ref, o_rows, *, shared_kv, dv):
    tq = ATTN_TQ
    for hh in range(2):
        rows = slice(hh * tq, (hh + 1) * tq)
        o = None
        off = 0
        for v_ref in v_refs:
            n = v_ref.shape[2]
            part = _dot(p_ref[rows, off:off + n], v_ref[0, 0 if shared_kv else hh])
            o = part if o is None else o + part
            off += n
        o_ref[0, o_rows, hh * dv:(hh + 1) * dv] = (o[:, :dv] / o[:, dv:]).astype(BF16)


def _attn_pipe_kernel(run_ref, q_ref, kt_lat, kt_ctx, v_lat, v_ctx, o_ref,
                      s_a, s_b, p_a, p_b, m_a, m_b, *, shared_kv, dk, dv):
    g = pl.program_id(0)

    @pl.when(g == 0)
    def _():
        s_b[...] = jnp.zeros_like(s_b)
        m_b[...] = jnp.zeros_like(m_b)
        p_a[...] = jnp.ones_like(p_a)

    kts, vs = [kt_lat, kt_ctx], [v_lat, v_ctx]

    def tick(rows, s_new, m_new, s_old, m_old, p_new, p_old):
        _scores(q_ref, rows, kts, s_new, m_new, shared_kv=shared_kv, dk=dk)
        _exp_scores(s_old, m_old, p_new)
        _weighted_values(p_old, vs, o_ref, rows, shared_kv=shared_kv, dv=dv)

    @pl.when(run_ref[0] == 1)
    def _():
        tick(slice(0, ATTN_TQ), s_a, m_a, s_b, m_b, p_b, p_a)

    @pl.when(run_ref[1] == 1)
    def _():
        tick(slice(ATTN_TQ, 2 * ATTN_TQ), s_b, m_b, s_a, m_a, p_a, p_b)


def _attn_ctx_kernel(q_ref, kt_ref, v_ref, o_ref, *, shared_kv, dk, dv):
    for hh in range(2):
        kv = 0 if shared_kv else hh
        s = _dot(q_ref[0, :, hh * dk:(hh + 1) * dk], kt_ref[0, kv])
        p = jnp.exp2(s - s.max(axis=-1, keepdims=True)).astype(BF16)
        o = _dot(p, v_ref[0, kv])
        o_ref[0, :, hh * dv:(hh + 1) * dv] = (o[:, :dv] / o[:, dv:]).astype(BF16)


def _attention(q, kt_lat, kt_ctx, v_lat, v_ctx, *, heads, kv_heads, dk, dv, name):
    bsz, lq, _ = q.shape
    shared_kv = kv_heads < heads
    group = heads // kv_heads
    n_pairs = heads // 2
    n_qb = lq // (2 * ATTN_TQ)
    n_steps = bsz * n_pairs * n_qb
    lag = 1
    kv_blk = 1 if shared_kv else 2
    n_lat, n_ctx = kt_lat.shape[-1], kt_ctx.shape[-1]

    def split(g):
        return g // (n_pairs * n_qb), (g // n_qb) % n_pairs, g % n_qb

    def q_map(g):
        b, p, i = split(jnp.minimum(g, n_steps - 1))
        return b, i, p

    def k_map(g):
        b, p, _ = split(jnp.minimum(g, n_steps - 1))
        return b, (2 * p) // group if shared_kv else p, 0, 0

    def v_map(g):
        b, p, _ = split(jnp.maximum(g - lag, 0))
        return b, (2 * p) // group if shared_kv else p, 0, 0

    def o_map(g):
        b, p, i = split(jnp.maximum(g - lag, 0))
        return b, i, p

    rows, keys = 2 * ATTN_TQ, n_lat + n_ctx
    return pl.pallas_call(
        functools.partial(_attn_pipe_kernel, shared_kv=shared_kv, dk=dk, dv=dv),
        grid=(n_steps + lag,),
        in_specs=[pl.BlockSpec(memory_space=pltpu.SMEM),
                  pl.BlockSpec((1, 2 * ATTN_TQ, 2 * dk), q_map),
                  pl.BlockSpec((1, kv_blk, dk, n_lat), k_map), pl.BlockSpec((1, kv_blk, dk, n_ctx), k_map),
                  pl.BlockSpec((1, kv_blk, n_lat, 2 * dv), v_map), pl.BlockSpec((1, kv_blk, n_ctx, 2 * dv), v_map)],
        out_specs=pl.BlockSpec((1, 2 * ATTN_TQ, 2 * dv), o_map),
        out_shape=jax.ShapeDtypeStruct((bsz, lq, heads * dv), BF16),
        scratch_shapes=[pltpu.VMEM((rows, keys), F32), pltpu.VMEM((rows, keys), F32),
                        pltpu.VMEM((rows, keys), BF16), pltpu.VMEM((rows, keys), BF16)]
                       + [pltpu.VMEM((rows, 1), F32)] * 2,
        compiler_params=_params(1),
        name=name,
    )(jnp.ones((2,), jnp.int32), q, kt_lat, kt_ctx, v_lat, v_ctx)


def _attention_ctx(q, kt, v, *, heads, kv_heads, dk, dv, name):
    bsz, lq, _ = q.shape
    shared_kv = kv_heads < heads
    group = heads // kv_heads
    kv_blk = 1 if shared_kv else 2
    kmap = (lambda b, p: (b, (2 * p) // group, 0, 0)) if shared_kv else (lambda b, p: (b, p, 0, 0))
    return pl.pallas_call(
        functools.partial(_attn_ctx_kernel, shared_kv=shared_kv, dk=dk, dv=dv),
        grid=(bsz, heads // 2),
        in_specs=[pl.BlockSpec((1, lq, 2 * dk), lambda b, p: (b, 0, p)),
                  pl.BlockSpec((1, kv_blk, dk, kt.shape[-1]), kmap),
                  pl.BlockSpec((1, kv_blk, v.shape[2], 2 * dv), kmap)],
        out_specs=pl.BlockSpec((1, lq, 2 * dv), lambda b, p: (b, 0, p)),
        out_shape=jax.ShapeDtypeStruct((bsz, lq, heads * dv), BF16),
        compiler_params=_params(2),
        name=name,
    )(q, kt, v)


def _s5_kernel(run_ref, u_ref, cin_ref, wb_ref, a_ref, apow_ref, wc_ref, yprev_ref, d_ref, wglu_ref,
               out_ref, cout_ref, bu_r, bu_i, xb, car, *, reverse, glu):
    @pl.when(pl.program_id(1) == 0)
    def _():
        car[...] = cin_ref[0]

    n_chunks = u_ref.shape[1] // S5_CHUNK
    for n, k in enumerate(range(n_chunks - 1, -1, -1) if reverse else range(n_chunks)):
        @pl.when(run_ref[n] == 1)
        def _(k=k):
            _s5_chunk(slice(k * S5_CHUNK, (k + 1) * S5_CHUNK), u_ref, wb_ref, a_ref, apow_ref, wc_ref, yprev_ref,
                      d_ref, wglu_ref, out_ref, bu_r, bu_i, xb, car, reverse=reverse, glu=glu)
    cout_ref[0] = car[...]


def _s5_chunk(tokens, u_ref, wb_ref, a_ref, apow_ref, wc_ref, yprev_ref, d_ref, wglu_ref,
              out_ref, bu_r, bu_i, xb, car, *, reverse, glu):
    ub = u_ref[0, tokens, :].astype(BF16)
    n_tiles = S5_CH // LANE
    tile_states = S5_LANES // n_tiles
    for t in range(n_tiles):
        r = _dot(ub[:, t * LANE:(t + 1) * LANE], wb_ref[t])
        bu_r[:, t * tile_states:(t + 1) * tile_states] = r[:, :tile_states]
        bu_i[:, t * tile_states:(t + 1) * tile_states] = r[:, tile_states:]

    steps = range(S5_SUB_LEN - 1, -1, -1) if reverse else range(S5_SUB_LEN)
    subs = range(S5_SUB - 1, -1, -1) if reverse else range(S5_SUB)
    pack_rows = 2 * S5_SUB
    for c in range(S5_LANES // S5_LANE_CHUNK):
        ls = slice(c * S5_LANE_CHUNK, (c + 1) * S5_LANE_CHUNK)
        ar = jnp.broadcast_to(a_ref[0:1, ls], (S5_SUB, S5_LANE_CHUNK))
        ai = jnp.broadcast_to(a_ref[1:2, ls], (S5_SUB, S5_LANE_CHUNK))
        xr = jnp.zeros((S5_SUB, S5_LANE_CHUNK), F32)
        xi = jnp.zeros((S5_SUB, S5_LANE_CHUNK), F32)
        for i in steps:
            rows = slice(i * S5_SUB, (i + 1) * S5_SUB)
            xr, xi = (ar * xr - ai * xi + bu_r[rows, ls], ar * xi + ai * xr + bu_i[rows, ls])
            bu_r[rows, ls] = xr
            bu_i[rows, ls] = xi
        cr, ci = car[0:1, ls], car[1:2, ls]
        a_sub_r, a_sub_i = a_ref[2:3, ls], a_ref[3:4, ls]
        crs, cis = [None] * S5_SUB, [None] * S5_SUB
        for j in subs:
            crs[j], cis[j] = cr, ci
            cr, ci = (a_sub_r * cr - a_sub_i * ci + xr[j:j + 1], a_sub_r * ci + a_sub_i * cr + xi[j:j + 1])
        car[0:1, ls] = cr
        car[1:2, ls] = ci
        cmr = jnp.concatenate(crs * (pack_rows // S5_SUB), axis=0)
        cmi = jnp.concatenate(cis * (pack_rows // S5_SUB), axis=0)
        col = (c * S5_LANE_CHUNK // tile_states) * 2 * tile_states + (c * S5_LANE_CHUNK) % tile_states
        for g in range(S5_CHUNK // pack_rows):
            rows = slice(g * pack_rows, (g + 1) * pack_rows)
            pr, pi = apow_ref[0, rows, ls], apow_ref[1, rows, ls]
            xb[rows, col:col + S5_LANE_CHUNK] = (bu_r[rows, ls] + (pr * cmr - pi * cmi)).astype(BF16)
            xb[rows, col + tile_states:col + tile_states + S5_LANE_CHUNK] = (
                bu_i[rows, ls] + (pr * cmi + pi * cmr)).astype(BF16)

    ys = [_dot(xb[:, t * 2 * tile_states:(t + 1) * 2 * tile_states], wc_ref[t]) for t in range(n_tiles)]
    y = jnp.concatenate(ys, axis=-1)
    if glu:
        y = y + yprev_ref[0, tokens, :] + d_ref[...] * u_ref[0, tokens, :]
        hg = _dot(jax.nn.gelu(y).astype(BF16), wglu_ref[...])
        out_ref[0, tokens, :] = (hg[:, :S5_CH] * jax.nn.sigmoid(hg[:, S5_CH:])).astype(out_ref.dtype)
    else:
        out_ref[0, tokens, :] = y


def _s5_scan(u, carry_in, sw, layer, yprev, d, w_glu, *, reverse, glu):
    bsz, n_tok, _ = u.shape
    per_step = min(S5_CHUNKS_PER_STEP, n_tok // S5_CHUNK)
    nc = n_tok // (per_step * S5_CHUNK)
    order = (lambda b, i: (b, nc - 1 - i, 0)) if reverse else (lambda b, i: (b, i, 0))
    tile_states = S5_LANES // (S5_CH // LANE)
    assert tile_states % S5_LANE_CHUNK == 0
    row_spec = pl.BlockSpec((1, per_step * S5_CHUNK, S5_CH), order)
    carry_spec = pl.BlockSpec((1, 2, S5_LANES), lambda b, i: (b, 0, 0))
    consts = [sw['wb'], sw['a'], sw['apow'], sw['wc']]
    out, carry = pl.pallas_call(
        functools.partial(_s5_kernel, reverse=reverse, glu=glu),
        grid=(bsz, nc),
        in_specs=[pl.BlockSpec(memory_space=pltpu.SMEM), row_spec, carry_spec]
                 + [_sel_spec(a, (layer, int(reverse))) for a in consts]
                 + [row_spec, _sel_spec(d, (layer,)), _sel_spec(w_glu, (layer,))],
        out_specs=[row_spec, carry_spec],
        out_shape=[jax.ShapeDtypeStruct((bsz, n_tok, S5_CH), BF16 if glu else F32),
                   jax.ShapeDtypeStruct((bsz, 2, S5_LANES), F32)],
        scratch_shapes=[pltpu.VMEM((S5_CHUNK, S5_LANES), F32), pltpu.VMEM((S5_CHUNK, S5_LANES), F32),
                        pltpu.VMEM((S5_CHUNK, 2 * S5_LANES), BF16), pltpu.VMEM((2, S5_LANES), F32)],
        compiler_params=_params(2),
        name="s5_bwd" if reverse else "s5_fwd",
    )(jnp.ones((per_step,), jnp.int32), u, carry_in, *consts, yprev, d, w_glu)
    return out, carry


def _merge_kernel(x_ref, mod_ref, ya_ref, ys_ref, yc_ref, wg_ref, wa_ref, ws_ref, wc_ref, wo_ref,
                  g_ref, b_ref, o_ref):
    x = x_ref[0]
    mod = mod_ref[0]
    h = (_layer_norm(x) * (1.0 + mod[1:2]) + mod[0:1]).astype(BF16)
    merged = None
    for t, (y_ref, w_ref) in enumerate(((ya_ref, wa_ref), (ys_ref, ws_ref), (yc_ref, wc_ref))):
        gate = jax.nn.sigmoid(_dot(h, wg_ref[:, t * D_MODEL:(t + 1) * D_MODEL]))
        term = gate * _dot(y_ref[0], w_ref[...])
        merged = term if merged is None else merged + term
    mix = _dot(merged.astype(BF16), wo_ref[...])
    y = DEEPNORM_ALPHA * x + mod[2:3] * mix
    o_ref[0] = _layer_norm(y) * g_ref[...] + b_ref[...]


def _mlp_kernel(x_ref, mod_ref, wu_ref, wd_ref, g_ref, b_ref, o_ref):
    x = x_ref[0]
    mod = mod_ref[0]
    h = (_layer_norm(x) * (1.0 + mod[4:5]) + mod[3:4]).astype(BF16)
    up = jnp.square(jnp.maximum(_dot(h, wu_ref[...]), 0.0)).astype(BF16)
    y = DEEPNORM_ALPHA * x + mod[5:6] * _dot(up, wd_ref[...])
    o_ref[0] = _layer_norm(y) * g_ref[...] + b_ref[...]


def _token_call(kernel, name, x, mod, rows, consts, layer, row_block):
    bsz, n_tok, _ = x.shape
    per_batch_mod = mod.shape[0] > 1
    rb = min(row_block, n_tok)

    def row_spec(width):
        return pl.BlockSpec((1, rb, width), lambda b, i: (b, i, 0))

    mod_spec = pl.BlockSpec((1, N_MOD, D_MODEL), (lambda b, i: (b, 0, 0)) if per_batch_mod else (lambda b, i: (0, 0, 0)))
    return pl.pallas_call(
        kernel,
        grid=(bsz, n_tok // rb),
        in_specs=[row_spec(D_MODEL), mod_spec] + [row_spec(r.shape[-1]) for r in rows]
                 + [_sel_spec(a, (layer,)) for a in consts],
        out_specs=row_spec(D_MODEL),
        out_shape=jax.ShapeDtypeStruct((bsz, n_tok, D_MODEL), F32),
        compiler_params=_params(2),
        name=name,
    )(x, mod, *rows, *consts)


def _axial_rope_tables(rows, dim):
    half = dim // 2
    inv = ROPE_THETA ** (-jnp.arange(0, half, 2, dtype=F32) / half)
    row = jnp.repeat(jnp.arange(rows, dtype=F32), GRID_W)
    col = jnp.tile(jnp.arange(GRID_W, dtype=F32), rows)
    ang_r = row[:, None] * inv
    ang_c = col[:, None] * inv
    ang = jnp.concatenate([ang_r, ang_r, ang_c, ang_c], axis=-1)
    return jnp.cos(ang), jnp.sin(ang)


def _rope_lane_tables(n_tok, dim, lane_off):
    cos, sin = _axial_rope_tables(n_tok // GRID_W, dim)
    reps = (LANE - lane_off) // dim if lane_off == 0 else 1
    cos_t = jnp.ones((n_tok, LANE), F32).at[:, lane_off:lane_off + reps * dim].set(jnp.tile(cos, (1, reps)))
    sin_t = jnp.zeros((n_tok, LANE), F32).at[:, lane_off:lane_off + reps * dim].set(jnp.tile(sin, (1, reps)))
    quarter = dim // 4
    first = (jnp.arange(LANE) % (2 * quarter)) < quarter
    return cos_t, jnp.where(first, -sin_t, 0.0), jnp.where(first, 0.0, sin_t)


def _block_diag(blocks):
    n, r, c = blocks.shape
    eye = jnp.eye(n, dtype=blocks.dtype)
    return (eye[:, None, :, None] * blocks[:, :, None, :]).reshape(n * r, n * c)


def _s5_discretize(a_re, a_im, log_dt, b_re, b_im):
    dt = jnp.exp(log_dt)[:, None]
    mag = jnp.exp(a_re * dt)
    abar_r = mag * jnp.cos(a_im * dt)
    abar_i = mag * jnp.sin(a_im * dt)
    den = a_re * a_re + a_im * a_im
    nr = abar_r - 1.0
    coef_r = (nr * a_re + abar_i * a_im) / den
    coef_i = (abar_i * a_re - nr * a_im) / den
    bbar_r = coef_r[..., None] * b_re - coef_i[..., None] * b_im
    bbar_i = coef_r[..., None] * b_im + coef_i[..., None] * b_re
    return abar_r, abar_i, bbar_r, bbar_i


def _s5_direction_weights(a_re, a_im, log_dt, b_re, b_im, c_re, c_im):
    abar_r, abar_i, bbar_r, bbar_i = _s5_discretize(a_re, a_im, log_dt, b_re, b_im)
    groups_per_tile = LANE // S5_GROUP_CH
    n_tiles = S5_GROUPS // groups_per_tile

    def in_tile(bbar):
        blk = jnp.swapaxes(bbar, 1, 2).reshape(n_tiles, groups_per_tile, S5_GROUP_CH, S5_STATE)
        return jax.vmap(_block_diag)(blk)

    def out_tile(cm):
        blk = jnp.swapaxes(cm, 1, 2).reshape(n_tiles, groups_per_tile, S5_STATE, S5_GROUP_CH)
        return jax.vmap(_block_diag)(blk)

    wb = jnp.concatenate([in_tile(bbar_r), in_tile(bbar_i)], axis=-1).astype(BF16)
    wc = jnp.concatenate([out_tile(c_re), -out_tile(c_im)], axis=1).astype(BF16)

    def step(carry, _):
        pr, pi = carry
        nxt = (pr * abar_r - pi * abar_i, pr * abar_i + pi * abar_r)
        return nxt, nxt
    _, (pows_r, pows_i) = lax.scan(step, (jnp.ones_like(abar_r), jnp.zeros_like(abar_r)), None, length=S5_SUB_LEN)
    pows = jnp.stack([pows_r.reshape(S5_SUB_LEN, S5_LANES), pows_i.reshape(S5_SUB_LEN, S5_LANES)])
    return wb, wc, jnp.stack([abar_r.reshape(-1), abar_i.reshape(-1)]), pows


def _s5_weights(a_re, a_im, log_dt, b_re, b_im, c_re, c_im):
    wb, wc, abar, pows = jax.vmap(jax.vmap(_s5_direction_weights))(a_re, a_im, log_dt, b_re, b_im, c_re, c_im)
    sub = pows[:, :, :, -1]
    pows = jnp.stack([pows[:, 0], pows[:, 1, :, ::-1]], axis=1)
    a = jnp.concatenate([abar, sub], axis=2)
    apow = jnp.repeat(pows, S5_SUB, axis=3)
    return {'wb': wb, 'a': a, 'apow': apow, 'wc': wc}


def _interleave(t):
    b, n, w = t.shape
    return t.reshape(b, n // S5_CHUNK, S5_SUB, S5_SUB_LEN, w).swapaxes(2, 3).reshape(b, n, w)


def _deinterleave(t):
    b, n, w = t.shape
    return t.reshape(b, n // S5_CHUNK, S5_SUB_LEN, S5_SUB, w).swapaxes(2, 3).reshape(b, n, w)


def _layer_weights(w_in, a_q_gain, a_k_gain, c_q_a_gain, c_kv_a_gain, c_w_qb, c_w_kvb):
    depth = w_in.shape[0]
    ckr = jnp.zeros((depth, D_MODEL, LANE), F32).at[:, :, C_NOPE:C_QK_DIM].set(w_in[:, :, OFF_CKR:OFF_U])
    w_re = jnp.concatenate([w_in[:, :, OFF_AK:OFF_CKR], ckr, w_in[:, :, OFF_U:OFF_GATE]], axis=2).astype(BF16)
    assert w_re.shape[2] == P_GATE
    no_pad = ((0, 0), (0, 0), (0, 0))
    qb = c_w_qb.reshape(depth, C_Q_RANK, C_HEADS, C_QK_DIM)
    qb = jnp.pad(qb, no_pad + ((0, C_HEAD_PAD - C_QK_DIM),)).reshape(depth, C_Q_RANK, C_HEADS * C_HEAD_PAD)
    kvb = c_w_kvb.reshape(depth, C_KV_RANK, C_HEADS, C_NOPE + C_VDIM)
    w_kc = jnp.pad(kvb[..., :C_NOPE], no_pad + ((0, C_HEAD_PAD - C_NOPE),))
    w_kc = w_kc.reshape(depth, C_KV_RANK, C_HEADS * C_HEAD_PAD)
    w_vc = kvb[..., C_NOPE:].reshape(depth, C_KV_RANK, C_HEADS * C_VDIM)
    heads_per_tile = LANE // A_HEAD_DIM
    ones = _block_diag(jnp.full((heads_per_tile, A_HEAD_DIM, A_HEAD_DIM), 1.0 / A_HEAD_DIM, F32))
    ones = jnp.concatenate([ones, ones], axis=0).astype(BF16)
    return {
        'w_in': w_re, 'w_gate': w_in[:, :, OFF_GATE:].astype(BF16),
        'w_kc': w_kc.astype(BF16), 'w_vc': w_vc.astype(BF16), 'w_qb': qb.astype(BF16),
        'ones': jnp.broadcast_to(ones[None], (depth,) + ones.shape),
        'gk': jnp.tile(a_k_gain, (1, heads_per_tile))[:, None, :],
        'gq': jnp.tile(a_q_gain, (1, heads_per_tile))[:, None, :] * (A_HEAD_DIM ** -0.5 * LOG2_E),
        'gkv': c_kv_a_gain[:, None, :], 'gcq': c_q_a_gain[:, None, :],
    }


def kernel(x, c, ctx, c_ctx, w_mod, b_mod, w_in, a_q_gain, a_k_gain, c_q_a_gain, c_kv_a_gain, c_w_qb, c_w_kvb, s5_a_re, s5_a_im, s5_log_dt, s5_b_re, s5_b_im, s5_c_re, s5_c_im, s5_d, s5_w_glu, w_branch_a, w_branch_s5, w_branch_c, w_out, ln1_g, ln1_b, w_up, w_down, ln2_g, ln2_b):
    bsz, seq, _ = x.shape
    tabs_lat = _rope_lane_tables(seq, A_HEAD_DIM, 0) + _rope_lane_tables(seq, C_ROPE, C_NOPE)
    c_all = jnp.zeros((8, D_MODEL), F32).at[:bsz].set(c).at[bsz].set(c_ctx)
    zero_carry = jnp.zeros((bsz, 2, S5_LANES), F32)

    lw = _layer_weights(w_in, a_q_gain, a_k_gain, c_q_a_gain, c_kv_a_gain, c_w_qb, c_w_kvb)
    sw = _s5_weights(s5_a_re, s5_a_im, s5_log_dt, s5_b_re, s5_b_im, s5_c_re, s5_c_im)
    d_rows = s5_d[:, None, :]
    w_glu = s5_w_glu.astype(BF16)
    merge_w = [lw['w_gate'], w_branch_a.astype(BF16), w_branch_s5.astype(BF16), w_branch_c.astype(BF16),
               w_out.astype(BF16), ln1_g[:, None, :], ln1_b[:, None, :]]
    mlp_w = [w_up.astype(BF16), w_down.astype(BF16), ln2_g[:, None, :], ln2_b[:, None, :]]

    for l in range(DEPTH):
        last = l == DEPTH - 1
        mod = _modulation(c_all, w_mod, b_mod, l).reshape(8, N_MOD, D_MODEL)
        mod_lat, mod_ctx = mod[:bsz], mod[bsz:bsz + 1]

        ka_c, va_c, kc_c, vc_c, u_c, qa_c, qc_c = _inproj(ctx, mod_ctx, lw, l, tabs_lat, rope=False)
        ka, va, kc, vc, u, qa, qc = _inproj(x, mod_lat, lw, l, tabs_lat, rope=True)

        u_c, u = _interleave(u_c), _interleave(u)
        yf_c, carry_f = _s5_scan(u_c, zero_carry, sw, l, u_c, d_rows, w_glu, reverse=False, glu=False)
        yf, _ = _s5_scan(u, carry_f, sw, l, u, d_rows, w_glu, reverse=False, glu=False)
        ys_c, carry_b = _s5_scan(u_c, zero_carry, sw, l, yf_c, d_rows, w_glu, reverse=True, glu=True)
        ys, _ = _s5_scan(u, carry_b, sw, l, yf, d_rows, w_glu, reverse=True, glu=True)
        ys_c, ys = _deinterleave(ys_c), _deinterleave(ys)

        ya = _attention(qa, ka, ka_c, va, va_c, heads=A_HEADS, kv_heads=A_KV_HEADS,
                        dk=A_HEAD_DIM, dv=A_HEAD_DIM, name="attn_a")
        yc = _attention(qc, kc, kc_c, vc, vc_c, heads=C_HEADS, kv_heads=C_HEADS,
                        dk=C_HEAD_PAD, dv=C_VDIM, name="attn_c")
        x_mid = _token_call(_merge_kernel, "merge", x, mod_lat, [ya, ys, yc], merge_w, l, MERGE_ROW_BLOCK)
        x_next = _token_call(_mlp_kernel, "mlp", x_mid, mod_lat, [], mlp_w, l, MLP_ROW_BLOCK)

        if not last:
            ya_c = _attention_ctx(qa_c, ka_c, va_c, heads=A_HEADS, kv_heads=A_KV_HEADS,
                                  dk=A_HEAD_DIM, dv=A_HEAD_DIM, name="attn_a_ctx")
            yc_c = _attention_ctx(qc_c, kc_c, vc_c, heads=C_HEADS, kv_heads=C_HEADS,
                                  dk=C_HEAD_PAD, dv=C_VDIM, name="attn_c_ctx")
            ctx_mid = _token_call(_merge_kernel, "merge_ctx", ctx, mod_ctx, [ya_c, ys_c, yc_c], merge_w, l,
                                  MERGE_ROW_BLOCK)
            ctx = _token_call(_mlp_kernel, "mlp_ctx", ctx_mid, mod_ctx, [], mlp_w, l, MLP_ROW_BLOCK)
        x = x_next
    return x
```

```python
import functools
import math

import jax
import jax.numpy as jnp
from jax import lax
from jax.experimental import pallas as pl
from jax.experimental.pallas import tpu as pltpu

F32 = jnp.float32
BF16 = jnp.bfloat16

D_MODEL = 1024
DEPTH = 2
GRID_W = 64
ROPE_THETA = 10000.0
EPS = 1e-6

A_HEADS = 8
A_KV_HEADS = 2
A_HEAD_DIM = 64
A_WIDTH = A_HEADS * A_HEAD_DIM

S5_CH = 512
S5_GROUP_CH = 16
S5_GROUPS = S5_CH // S5_GROUP_CH
S5_STATE = 64
S5_LANES = S5_GROUPS * S5_STATE

C_HEADS = 8
C_NOPE = 64
C_ROPE = 32
C_VDIM = 64
C_Q_RANK = 768
C_KV_RANK = 256
C_QK_DIM = C_NOPE + C_ROPE

N_BRANCH = 3
N_MOD = 6
DEEPNORM_ALPHA = (2.0 * DEPTH) ** 0.25
LOG2_E = math.log2(math.e)

OFF_AK = 0
OFF_AV = OFF_AK + A_KV_HEADS * A_HEAD_DIM
OFF_CKV = OFF_AV + A_KV_HEADS * A_HEAD_DIM
OFF_CKR = OFF_CKV + C_KV_RANK
OFF_U = OFF_CKR + C_ROPE
OFF_AQ = OFF_U + S5_CH
OFF_CQ = OFF_AQ + A_WIDTH
OFF_GATE = OFF_CQ + C_Q_RANK

LANE = 128
C_HEAD_PAD = LANE

P_AK = 0
P_AV = P_AK + LANE
P_CKV = P_AV + LANE
P_CKR = P_CKV + C_KV_RANK
P_U = P_CKR + LANE
P_AQ = P_U + S5_CH
P_CQ = P_AQ + A_WIDTH
P_GATE = P_CQ + C_Q_RANK

MLP_ROW_BLOCK = 512
MERGE_ROW_BLOCK = 1024
INPROJ_ROW_BLOCK = 1024
ATTN_TQ = 256
ATTN_KEY_CHUNK = 512
S5_CHUNK = 256
S5_SUB = 8
S5_SUB_LEN = S5_CHUNK // S5_SUB
S5_LANE_CHUNK = 512
VMEM_LIMIT = 56 * 1024 * 1024


def _sel_spec(arr, idx):
    nd = arr.ndim - len(idx)
    return pl.BlockSpec((None,) * len(idx) + tuple(arr.shape[len(idx):]), lambda *_: tuple(idx) + (0,) * nd,
                        pipeline_mode=pl.Buffered(1))


def _params(n_grid):
    return pltpu.CompilerParams(dimension_semantics=("arbitrary",) * n_grid,
                                vmem_limit_bytes=VMEM_LIMIT)


def _layer_norm(x):
    mu = jnp.mean(x, axis=-1, keepdims=True)
    xc = x - mu
    var = jnp.mean(xc * xc, axis=-1, keepdims=True)
    return xc * lax.rsqrt(var + EPS)


def _dot(a, b):
    return jnp.dot(a, b, preferred_element_type=F32)


def _mod_kernel(c_ref, w_ref, b_ref, o_ref):
    s = jax.nn.silu(c_ref[...]).astype(BF16)
    o_ref[...] = _dot(s, w_ref[...].astype(BF16)) + b_ref[...]


def _modulation(c_all, w_mod, b_mod, layer):
    n = N_MOD * D_MODEL
    bn = n // 4
    return pl.pallas_call(
        _mod_kernel,
        grid=(n // bn,),
        in_specs=[pl.BlockSpec((8, D_MODEL), lambda j: (0, 0)),
                  pl.BlockSpec((None, D_MODEL, bn), lambda j: (layer, 0, j)),
                  pl.BlockSpec((None, 1, bn), lambda j: (layer, 0, j))],
        out_specs=pl.BlockSpec((8, bn), lambda j: (0, j)),
        out_shape=jax.ShapeDtypeStruct((8, n), F32),
        compiler_params=_params(1),
        name="mod",
    )(c_all, w_mod, b_mod.reshape(DEPTH, 1, n))


def _rope(x, cos, sin_next, sin_prev, shift):
    n = x.shape[-1]
    return x * cos + pltpu.roll(x, n - shift, 1) * sin_next + pltpu.roll(x, shift, 1) * sin_prev


def _head_rms_norm(p, ones_ref, gain):
    sq = p * p
    hi = sq.astype(BF16)
    lo = (sq - hi.astype(F32)).astype(BF16)
    ms = _dot(jnp.concatenate([hi, lo], axis=1), ones_ref[...])
    return p * lax.rsqrt(ms + EPS) * gain


def _row_rms_norm(p, gain):
    return p * lax.rsqrt(jnp.mean(p * p, axis=-1, keepdims=True) + EPS) * gain


def _value_tile(v, head):
    half = LANE // 2
    t = v[:, (head // 2) * LANE:(head // 2 + 1) * LANE]
    if head % 2:
        t = pltpu.roll(t, half, 1)
    low = lax.broadcasted_iota(jnp.int32, t.shape, 1) < half
    return jnp.where(low, t, 1.0).astype(BF16)


def _inproj_kernel(x_ref, mod_ref, w_ref, wkc_ref, wvc_ref, wqb_ref, ones_ref,
                   gk_ref, gq_ref, gkv_ref, gcq_ref,
                   cosa_ref, sna_ref, spa_ref, cosc_ref, snc_ref, spc_ref,
                   ka_ref, va_ref, kc_ref, vc_ref, u_ref, qa_ref, qc_ref, *, rope):
    mod = mod_ref[0]
    h = (_layer_norm(x_ref[0]) * (1.0 + mod[1:2]) + mod[0:1]).astype(BF16)

    def proj(a, b):
        return _dot(h, w_ref[:, a:b])

    def rope_a(t):
        return _rope(t, cosa_ref[...], sna_ref[...], spa_ref[...], A_HEAD_DIM // 4) if rope else t

    def rope_c(t):
        return _rope(t, cosc_ref[...], snc_ref[...], spc_ref[...], C_ROPE // 4) if rope else t

    state = proj(P_AK, P_U)
    kt = rope_a(_head_rms_norm(state[:, P_AK:P_AV], ones_ref, gk_ref[...])).T.astype(BF16)
    v = state[:, P_AV:P_CKV]
    for hh in range(A_KV_HEADS):
        ka_ref[0, hh] = kt[hh * A_HEAD_DIM:(hh + 1) * A_HEAD_DIM, :]
        va_ref[0, hh] = _value_tile(v, hh)

    ckv = _row_rms_norm(state[:, P_CKV:P_CKR], gkv_ref[...]).astype(BF16)
    k_rope = rope_c(state[:, P_CKR:P_U])
    k_nope = _dot(ckv, wkc_ref[...])
    vc = _dot(ckv, wvc_ref[...])
    for hh in range(C_HEADS):
        kc_ref[0, hh] = (k_nope[:, hh * C_HEAD_PAD:(hh + 1) * C_HEAD_PAD] + k_rope).T.astype(BF16)
        vc_ref[0, hh] = _value_tile(vc, hh)

    u_ref[0] = proj(P_U, P_AQ)

    aq = proj(P_AQ, P_CQ)
    for t in range(A_WIDTH // LANE):
        q = _head_rms_norm(aq[:, t * LANE:(t + 1) * LANE], ones_ref, gq_ref[...])
        qa_ref[0, :, t * LANE:(t + 1) * LANE] = rope_a(q).astype(BF16)

    cq = _row_rms_norm(proj(P_CQ, P_GATE), gcq_ref[...]).astype(BF16)
    qc = _dot(cq, wqb_ref[...])
    for hh in range(C_HEADS):
        q = rope_c(qc[:, hh * C_HEAD_PAD:(hh + 1) * C_HEAD_PAD])
        qc_ref[0, :, hh * C_HEAD_PAD:(hh + 1) * C_HEAD_PAD] = (q * (C_QK_DIM ** -0.5 * LOG2_E)).astype(BF16)


def _inproj(x, mod, lw, layer, tabs, rope):
    bsz, n_tok, _ = x.shape
    rb = min(INPROJ_ROW_BLOCK, n_tok)
    nb = n_tok // rb
    per_batch_mod = mod.shape[0] > 1

    def row_spec(width):
        return pl.BlockSpec((1, rb, width), lambda b, i: (b, i, 0))

    def head_spec(heads, width):
        return pl.BlockSpec((1, heads, rb, width), lambda b, i: (b, 0, i, 0))

    tab_spec = pl.BlockSpec((rb, LANE), lambda b, i: (i, 0))
    mod_spec = pl.BlockSpec((1, N_MOD, D_MODEL), (lambda b, i: (b, 0, 0)) if per_batch_mod else (lambda b, i: (0, 0, 0)))
    consts = [lw['w_in'], lw['w_kc'], lw['w_vc'], lw['w_qb'], lw['ones'],
              lw['gk'], lw['gq'], lw['gkv'], lw['gcq']]
    def head_t_spec(heads, width):
        return pl.BlockSpec((1, heads, width, rb), lambda b, i: (b, 0, 0, i))

    out_shape = [
        jax.ShapeDtypeStruct((bsz, A_KV_HEADS, A_HEAD_DIM, n_tok), BF16),
        jax.ShapeDtypeStruct((bsz, A_KV_HEADS, n_tok, 2 * A_HEAD_DIM), BF16),
        jax.ShapeDtypeStruct((bsz, C_HEADS, C_HEAD_PAD, n_tok), BF16),
        jax.ShapeDtypeStruct((bsz, C_HEADS, n_tok, 2 * C_VDIM), BF16),
        jax.ShapeDtypeStruct((bsz, n_tok, S5_CH), F32),
        jax.ShapeDtypeStruct((bsz, n_tok, A_WIDTH), BF16),
        jax.ShapeDtypeStruct((bsz, n_tok, C_HEADS * C_HEAD_PAD), BF16),
    ]
    out_specs = [head_t_spec(A_KV_HEADS, A_HEAD_DIM), head_spec(A_KV_HEADS, 2 * A_HEAD_DIM),
                 head_t_spec(C_HEADS, C_HEAD_PAD), head_spec(C_HEADS, 2 * C_VDIM),
                 row_spec(S5_CH), row_spec(A_WIDTH), row_spec(C_HEADS * C_HEAD_PAD)]
    return pl.pallas_call(
        functools.partial(_inproj_kernel, rope=rope),
        grid=(bsz, nb),
        in_specs=[row_spec(D_MODEL), mod_spec] + [_sel_spec(a, (layer,)) for a in consts] + [tab_spec] * 6,
        out_specs=out_specs,
        out_shape=out_shape,
        compiler_params=_params(2),
        name="inproj",
    )(x, mod, *consts, *tabs)


def _scores(q_ref, q_rows, kt_refs, s_ref, m_ref, *, shared_kv, dk):
    tq = ATTN_TQ
    q = q_ref[0, q_rows, :]
    if shared_kv:
        lhs = [(slice(0, 2 * tq), jnp.concatenate([q[:, :dk], q[:, dk:]], axis=0), 0)]
    else:
        lhs = [(slice(hh * tq, (hh + 1) * tq), q[:, hh * dk:(hh + 1) * dk], hh) for hh in range(2)]
    for rows, qh, hh in lhs:
        m = None
        off = 0
        for kt_ref in kt_refs:
            n = kt_ref.shape[-1]
            for c0 in range(0, n, ATTN_KEY_CHUNK):
                c1 = min(c0 + ATTN_KEY_CHUNK, n)
                s = _dot(qh, kt_ref[0, hh, :, c0:c1])
                s_ref[rows, off + c0:off + c1] = s
                mc = s.max(axis=-1, keepdims=True)
                m = mc if m is None else jnp.maximum(m, mc)
            off += n
        m_ref[rows] = m


def _exp_scores(s_ref, m_ref, p_ref):
    m = m_ref[...]
    n = s_ref.shape[-1]
    for c0 in range(0, n, ATTN_KEY_CHUNK):
        c1 = min(c0 + ATTN_KEY_CHUNK, n)
        p_ref[:, c0:c1] = jnp.exp2(s_ref[:, c0:c1] - m).astype(BF16)


def _weighted_values(p_ref, v_refs, o_ref, o_rows, *, shared_kv, dv):
    tq = ATTN_TQ
    for hh in range(2):
        rows = slice(hh * tq, (hh + 1) * tq)
        o = None
        off = 0
        for v_ref in v_refs:
            n = v_ref.shape[2]
            part = _dot(p_ref[rows, off:off + n], v_ref[0, 0 if shared_kv else hh])
            o = part if o is None else o + part
            off += n
        o_ref[0, o_rows, hh * dv:(hh + 1) * dv] = (o[:, :dv] / o[:, dv:]).astype(BF16)


def _attn_pipe_kernel(run_ref, q_ref, kt_lat, kt_ctx, v_lat, v_ctx, o_ref,
                      s_a, s_b, p_a, p_b, m_a, m_b, *, shared_kv, dk, dv):
    g = pl.program_id(0)

    @pl.when(g == 0)
    def _():
        s_b[...] = jnp.zeros_like(s_b)
        m_b[...] = jnp.zeros_like(m_b)
        p_a[...] = jnp.ones_like(p_a)

    kts, vs = [kt_lat, kt_ctx], [v_lat, v_ctx]

    def tick(rows, s_new, m_new, s_old, m_old, p_new, p_old):
        _scores(q_ref, rows, kts, s_new, m_new, shared_kv=shared_kv, dk=dk)
        _exp_scores(s_old, m_old, p_new)
        _weighted_values(p_old, vs, o_ref, rows, shared_kv=shared_kv, dv=dv)

    @pl.when(run_ref[0] == 1)
    def _():
        tick(slice(0, ATTN_TQ), s_a, m_a, s_b, m_b, p_b, p_a)

    @pl.when(run_ref[1] == 1)
    def _():
        tick(slice(ATTN_TQ, 2 * ATTN_TQ), s_b, m_b, s_a, m_a, p_a, p_b)


def _attn_ctx_kernel(q_ref, kt_ref, v_ref, o_ref, *, shared_kv, dk, dv):
    for hh in range(2):
        kv = 0 if shared_kv else hh
        s = _dot(q_ref[0, :, hh * dk:(hh + 1) * dk], kt_ref[0, kv])
        p = jnp.exp2(s - s.max(axis=-1, keepdims=True)).astype(BF16)
        o = _dot(p, v_ref[0, kv])
        o_ref[0, :, hh * dv:(hh + 1) * dv] = (o[:, :dv] / o[:, dv:]).astype(BF16)


def _attention(q, kt_lat, kt_ctx, v_lat, v_ctx, *, heads, kv_heads, dk, dv, name):
    bsz, lq, _ = q.shape
    shared_kv = kv_heads < heads
    group = heads // kv_heads
    n_pairs = heads // 2
    n_qb = lq // (2 * ATTN_TQ)
    n_steps = bsz * n_pairs * n_qb
    lag = 1
    kv_blk = 1 if shared_kv else 2
    n_lat, n_ctx = kt_lat.shape[-1], kt_ctx.shape[-1]

    def split(g):
        return g // (n_pairs * n_qb), (g // n_qb) % n_pairs, g % n_qb

    def q_map(g):
        b, p, i = split(jnp.minimum(g, n_steps - 1))
        return b, i, p

    def k_map(g):
        b, p, _ = split(jnp.minimum(g, n_steps - 1))
        return b, (2 * p) // group if shared_kv else p, 0, 0

    def v_map(g):
        b, p, _ = split(jnp.maximum(g - lag, 0))
        return b, (2 * p) // group if shared_kv else p, 0, 0

    def o_map(g):
        b, p, i = split(jnp.maximum(g - lag, 0))
        return b, i, p

    rows, keys = 2 * ATTN_TQ, n_lat + n_ctx
    return pl.pallas_call(
        functools.partial(_attn_pipe_kernel, shared_kv=shared_kv, dk=dk, dv=dv),
        grid=(n_steps + lag,),
        in_specs=[pl.BlockSpec(memory_space=pltpu.SMEM),
                  pl.BlockSpec((1, 2 * ATTN_TQ, 2 * dk), q_map),
                  pl.BlockSpec((1, kv_blk, dk, n_lat), k_map), pl.BlockSpec((1, kv_blk, dk, n_ctx), k_map),
                  pl.BlockSpec((1, kv_blk, n_lat, 2 * dv), v_map), pl.BlockSpec((1, kv_blk, n_ctx, 2 * dv), v_map)],
        out_specs=pl.BlockSpec((1, 2 * ATTN_TQ, 2 * dv), o_map),
        out_shape=jax.ShapeDtypeStruct((bsz, lq, heads * dv), BF16),
        scratch_shapes=[pltpu.VMEM((rows, keys), F32), pltpu.VMEM((rows, keys), F32),
                        pltpu.VMEM((rows, keys), BF16), pltpu.VMEM((rows, keys), BF16)]
                       + [pltpu.VMEM((rows, 1), F32)] * 2,
        compiler_params=_params(1),
        name=name,
    )(jnp.ones((2,), jnp.int32), q, kt_lat, kt_ctx, v_lat, v_ctx)


def _attention_ctx(q, kt, v, *, heads, kv_heads, dk, dv, name):
    bsz, lq, _ = q.shape
    shared_kv = kv_heads < heads
    group = heads // kv_heads
    kv_blk = 1 if shared_kv else 2
    kmap = (lambda b, p: (b, (2 * p) // group, 0, 0)) if shared_kv else (lambda b, p: (b, p, 0, 0))
    return pl.pallas_call(
        functools.partial(_attn_ctx_kernel, shared_kv=shared_kv, dk=dk, dv=dv),
        grid=(bsz, heads // 2),
        in_specs=[pl.BlockSpec((1, lq, 2 * dk), lambda b, p: (b, 0, p)),
                  pl.BlockSpec((1, kv_blk, dk, kt.shape[-1]), kmap),
                  pl.BlockSpec((1, kv_blk, v.shape[2], 2 * dv), kmap)],
        out_specs=pl.BlockSpec((1, lq, 2 * dv), lambda b, p: (b, 0, p)),
        out_shape=jax.ShapeDtypeStruct((bsz, lq, heads * dv), BF16),
        compiler_params=_params(2),
        name=name,
    )(q, kt, v)


def _s5_kernel(u_ref, cin_ref, wb_ref, a_ref, apow_ref, wc_ref, yprev_ref, d_ref, wglu_ref,
               out_ref, cout_ref, bu_r, bu_i, xb, car, *, reverse, glu):
    @pl.when(pl.program_id(1) == 0)
    def _():
        car[...] = cin_ref[0]

    ub = u_ref[0].astype(BF16)
    n_tiles = S5_CH // LANE
    tile_states = S5_LANES // n_tiles
    for t in range(n_tiles):
        r = _dot(ub[:, t * LANE:(t + 1) * LANE], wb_ref[t])
        bu_r[:, t * tile_states:(t + 1) * tile_states] = r[:, :tile_states]
        bu_i[:, t * tile_states:(t + 1) * tile_states] = r[:, tile_states:]

    steps = range(S5_SUB_LEN - 1, -1, -1) if reverse else range(S5_SUB_LEN)
    subs = range(S5_SUB - 1, -1, -1) if reverse else range(S5_SUB)
    pack_rows = 2 * S5_SUB
    for c in range(S5_LANES // S5_LANE_CHUNK):
        ls = slice(c * S5_LANE_CHUNK, (c + 1) * S5_LANE_CHUNK)
        ar = jnp.broadcast_to(a_ref[0:1, ls], (S5_SUB, S5_LANE_CHUNK))
        ai = jnp.broadcast_to(a_ref[1:2, ls], (S5_SUB, S5_LANE_CHUNK))
        xr = jnp.zeros((S5_SUB, S5_LANE_CHUNK), F32)
        xi = jnp.zeros((S5_SUB, S5_LANE_CHUNK), F32)
        for i in steps:
            rows = slice(i * S5_SUB, (i + 1) * S5_SUB)
            xr, xi = (ar * xr - ai * xi + bu_r[rows, ls], ar * xi + ai * xr + bu_i[rows, ls])
            bu_r[rows, ls] = xr
            bu_i[rows, ls] = xi
        cr, ci = car[0:1, ls], car[1:2, ls]
        a_sub_r, a_sub_i = a_ref[2:3, ls], a_ref[3:4, ls]
        crs, cis = [None] * S5_SUB, [None] * S5_SUB
        for j in subs:
            crs[j], cis[j] = cr, ci
            cr, ci = (a_sub_r * cr - a_sub_i * ci + xr[j:j + 1], a_sub_r * ci + a_sub_i * cr + xi[j:j + 1])
        car[0:1, ls] = cr
        car[1:2, ls] = ci
        cmr = jnp.concatenate(crs * (pack_rows // S5_SUB), axis=0)
        cmi = jnp.concatenate(cis * (pack_rows // S5_SUB), axis=0)
        col = (c * S5_LANE_CHUNK // tile_states) * 2 * tile_states + (c * S5_LANE_CHUNK) % tile_states
        for g in range(S5_CHUNK // pack_rows):
            rows = slice(g * pack_rows, (g + 1) * pack_rows)
            pr, pi = apow_ref[0, rows, ls], apow_ref[1, rows, ls]
            xb[rows, col:col + S5_LANE_CHUNK] = (bu_r[rows, ls] + (pr * cmr - pi * cmi)).astype(BF16)
            xb[rows, col + tile_states:col + tile_states + S5_LANE_CHUNK] = (
                bu_i[rows, ls] + (pr * cmi + pi * cmr)).astype(BF16)
    cout_ref[0] = car[...]

    ys = [_dot(xb[:, t * 2 * tile_states:(t + 1) * 2 * tile_states], wc_ref[t]) for t in range(n_tiles)]
    y = jnp.concatenate(ys, axis=-1)
    if glu:
        y = y + yprev_ref[0] + d_ref[...] * u_ref[0]
        hg = _dot(jax.nn.gelu(y).astype(BF16), wglu_ref[...])
        out_ref[0] = (hg[:, :S5_CH] * jax.nn.sigmoid(hg[:, S5_CH:])).astype(out_ref.dtype)
    else:
        out_ref[0] = y


def _s5_scan(u, carry_in, sw, layer, yprev, d, w_glu, *, reverse, glu):
    bsz, n_tok, _ = u.shape
    nc = n_tok // S5_CHUNK
    order = (lambda b, i: (b, nc - 1 - i, 0)) if reverse else (lambda b, i: (b, i, 0))
    tile_states = S5_LANES // (S5_CH // LANE)
    assert tile_states % S5_LANE_CHUNK == 0
    row_spec = pl.BlockSpec((1, S5_CHUNK, S5_CH), order)
    carry_spec = pl.BlockSpec((1, 2, S5_LANES), lambda b, i: (b, 0, 0))
    consts = [sw['wb'], sw['a'], sw['apow'], sw['wc']]
    out, carry = pl.pallas_call(
        functools.partial(_s5_kernel, reverse=reverse, glu=glu),
        grid=(bsz, nc),
        in_specs=[row_spec, carry_spec] + [_sel_spec(a, (layer, int(reverse))) for a in consts]
                 + [row_spec, _sel_spec(d, (layer,)), _sel_spec(w_glu, (layer,))],
        out_specs=[row_spec, carry_spec],
        out_shape=[jax.ShapeDtypeStruct((bsz, n_tok, S5_CH), BF16 if glu else F32),
                   jax.ShapeDtypeStruct((bsz, 2, S5_LANES), F32)],
        scratch_shapes=[pltpu.VMEM((S5_CHUNK, S5_LANES), F32), pltpu.VMEM((S5_CHUNK, S5_LANES), F32),
                        pltpu.VMEM((S5_CHUNK, 2 * S5_LANES), BF16), pltpu.VMEM((2, S5_LANES), F32)],
        compiler_params=_params(2),
        name="s5_bwd" if reverse else "s5_fwd",
    )(u, carry_in, *consts, yprev, d, w_glu)
    return out, carry


def _merge_kernel(x_ref, mod_ref, ya_ref, ys_ref, yc_ref, wg_ref, wa_ref, ws_ref, wc_ref, wo_ref,
                  g_ref, b_ref, o_ref):
    x = x_ref[0]
    mod = mod_ref[0]
    h = (_layer_norm(x) * (1.0 + mod[1:2]) + mod[0:1]).astype(BF16)
    merged = None
    for t, (y_ref, w_ref) in enumerate(((ya_ref, wa_ref), (ys_ref, ws_ref), (yc_ref, wc_ref))):
        gate = jax.nn.sigmoid(_dot(h, wg_ref[:, t * D_MODEL:(t + 1) * D_MODEL]))
        term = gate * _dot(y_ref[0], w_ref[...])
        merged = term if merged is None else merged + term
    mix = _dot(merged.astype(BF16), wo_ref[...])
    y = DEEPNORM_ALPHA * x + mod[2:3] * mix
    o_ref[0] = _layer_norm(y) * g_ref[...] + b_ref[...]


def _mlp_kernel(x_ref, mod_ref, wu_ref, wd_ref, g_ref, b_ref, o_ref):
    x = x_ref[0]
    mod = mod_ref[0]
    h = (_layer_norm(x) * (1.0 + mod[4:5]) + mod[3:4]).astype(BF16)
    up = jnp.square(jnp.maximum(_dot(h, wu_ref[...]), 0.0)).astype(BF16)
    y = DEEPNORM_ALPHA * x + mod[5:6] * _dot(up, wd_ref[...])
    o_ref[0] = _layer_norm(y) * g_ref[...] + b_ref[...]


def _token_call(kernel, name, x, mod, rows, consts, layer, row_block):
    bsz, n_tok, _ = x.shape
    per_batch_mod = mod.shape[0] > 1
    rb = min(row_block, n_tok)

    def row_spec(width):
        return pl.BlockSpec((1, rb, width), lambda b, i: (b, i, 0))

    mod_spec = pl.BlockSpec((1, N_MOD, D_MODEL), (lambda b, i: (b, 0, 0)) if per_batch_mod else (lambda b, i: (0, 0, 0)))
    return pl.pallas_call(
        kernel,
        grid=(bsz, n_tok // rb),
        in_specs=[row_spec(D_MODEL), mod_spec] + [row_spec(r.shape[-1]) for r in rows]
                 + [_sel_spec(a, (layer,)) for a in consts],
        out_specs=row_spec(D_MODEL),
        out_shape=jax.ShapeDtypeStruct((bsz, n_tok, D_MODEL), F32),
        compiler_params=_params(2),
        name=name,
    )(x, mod, *rows, *consts)


def _axial_rope_tables(rows, dim):
    half = dim // 2
    inv = ROPE_THETA ** (-jnp.arange(0, half, 2, dtype=F32) / half)
    row = jnp.repeat(jnp.arange(rows, dtype=F32), GRID_W)
    col = jnp.tile(jnp.arange(GRID_W, dtype=F32), rows)
    ang_r = row[:, None] * inv
    ang_c = col[:, None] * inv
    ang = jnp.concatenate([ang_r, ang_r, ang_c, ang_c], axis=-1)
    return jnp.cos(ang), jnp.sin(ang)


def _rope_lane_tables(n_tok, dim, lane_off):
    cos, sin = _axial_rope_tables(n_tok // GRID_W, dim)
    reps = (LANE - lane_off) // dim if lane_off == 0 else 1
    cos_t = jnp.ones((n_tok, LANE), F32).at[:, lane_off:lane_off + reps * dim].set(jnp.tile(cos, (1, reps)))
    sin_t = jnp.zeros((n_tok, LANE), F32).at[:, lane_off:lane_off + reps * dim].set(jnp.tile(sin, (1, reps)))
    quarter = dim // 4
    first = (jnp.arange(LANE) % (2 * quarter)) < quarter
    return cos_t, jnp.where(first, -sin_t, 0.0), jnp.where(first, 0.0, sin_t)


def _block_diag(blocks):
    n, r, c = blocks.shape
    eye = jnp.eye(n, dtype=blocks.dtype)
    return (eye[:, None, :, None] * blocks[:, :, None, :]).reshape(n * r, n * c)


def _s5_discretize(a_re, a_im, log_dt, b_re, b_im):
    dt = jnp.exp(log_dt)[:, None]
    mag = jnp.exp(a_re * dt)
    abar_r = mag * jnp.cos(a_im * dt)
    abar_i = mag * jnp.sin(a_im * dt)
    den = a_re * a_re + a_im * a_im
    nr = abar_r - 1.0
    coef_r = (nr * a_re + abar_i * a_im) / den
    coef_i = (abar_i * a_re - nr * a_im) / den
    bbar_r = coef_r[..., None] * b_re - coef_i[..., None] * b_im
    bbar_i = coef_r[..., None] * b_im + coef_i[..., None] * b_re
    return abar_r, abar_i, bbar_r, bbar_i


def _s5_direction_weights(a_re, a_im, log_dt, b_re, b_im, c_re, c_im):
    abar_r, abar_i, bbar_r, bbar_i = _s5_discretize(a_re, a_im, log_dt, b_re, b_im)
    groups_per_tile = LANE // S5_GROUP_CH
    n_tiles = S5_GROUPS // groups_per_tile

    def in_tile(bbar):
        blk = jnp.swapaxes(bbar, 1, 2).reshape(n_tiles, groups_per_tile, S5_GROUP_CH, S5_STATE)
        return jax.vmap(_block_diag)(blk)

    def out_tile(cm):
        blk = jnp.swapaxes(cm, 1, 2).reshape(n_tiles, groups_per_tile, S5_STATE, S5_GROUP_CH)
        return jax.vmap(_block_diag)(blk)

    wb = jnp.concatenate([in_tile(bbar_r), in_tile(bbar_i)], axis=-1).astype(BF16)
    wc = jnp.concatenate([out_tile(c_re), -out_tile(c_im)], axis=1).astype(BF16)

    def step(carry, _):
        pr, pi = carry
        nxt = (pr * abar_r - pi * abar_i, pr * abar_i + pi * abar_r)
        return nxt, nxt
    _, (pows_r, pows_i) = lax.scan(step, (jnp.ones_like(abar_r), jnp.zeros_like(abar_r)), None, length=S5_SUB_LEN)
    pows = jnp.stack([pows_r.reshape(S5_SUB_LEN, S5_LANES), pows_i.reshape(S5_SUB_LEN, S5_LANES)])
    return wb, wc, jnp.stack([abar_r.reshape(-1), abar_i.reshape(-1)]), pows


def _s5_weights(a_re, a_im, log_dt, b_re, b_im, c_re, c_im):
    wb, wc, abar, pows = jax.vmap(jax.vmap(_s5_direction_weights))(a_re, a_im, log_dt, b_re, b_im, c_re, c_im)
    sub = pows[:, :, :, -1]
    pows = jnp.stack([pows[:, 0], pows[:, 1, :, ::-1]], axis=1)
    a = jnp.concatenate([abar, sub], axis=2)
    apow = jnp.repeat(pows, S5_SUB, axis=3)
    return {'wb': wb, 'a': a, 'apow': apow, 'wc': wc}


def _interleave(t):
    b, n, w = t.shape
    return t.reshape(b, n // S5_CHUNK, S5_SUB, S5_SUB_LEN, w).swapaxes(2, 3).reshape(b, n, w)


def _deinterleave(t):
    b, n, w = t.shape
    return t.reshape(b, n // S5_CHUNK, S5_SUB_LEN, S5_SUB, w).swapaxes(2, 3).reshape(b, n, w)


def _layer_weights(w_in, a_q_gain, a_k_gain, c_q_a_gain, c_kv_a_gain, c_w_qb, c_w_kvb):
    depth = w_in.shape[0]
    ckr = jnp.zeros((depth, D_MODEL, LANE), F32).at[:, :, C_NOPE:C_QK_DIM].set(w_in[:, :, OFF_CKR:OFF_U])
    w_re = jnp.concatenate([w_in[:, :, OFF_AK:OFF_CKR], ckr, w_in[:, :, OFF_U:OFF_GATE]], axis=2).astype(BF16)
    assert w_re.shape[2] == P_GATE
    no_pad = ((0, 0), (0, 0), (0, 0))
    qb = c_w_qb.reshape(depth, C_Q_RANK, C_HEADS, C_QK_DIM)
    qb = jnp.pad(qb, no_pad + ((0, C_HEAD_PAD - C_QK_DIM),)).reshape(depth, C_Q_RANK, C_HEADS * C_HEAD_PAD)
    kvb = c_w_kvb.reshape(depth, C_KV_RANK, C_HEADS, C_NOPE + C_VDIM)
    w_kc = jnp.pad(kvb[..., :C_NOPE], no_pad + ((0, C_HEAD_PAD - C_NOPE),))
    w_kc = w_kc.reshape(depth, C_KV_RANK, C_HEADS * C_HEAD_PAD)
    w_vc = kvb[..., C_NOPE:].reshape(depth, C_KV_RANK, C_HEADS * C_VDIM)
    heads_per_tile = LANE // A_HEAD_DIM
    ones = _block_diag(jnp.full((heads_per_tile, A_HEAD_DIM, A_HEAD_DIM), 1.0 / A_HEAD_DIM, F32))
    ones = jnp.concatenate([ones, ones], axis=0).astype(BF16)
    return {
        'w_in': w_re, 'w_gate': w_in[:, :, OFF_GATE:].astype(BF16),
        'w_kc': w_kc.astype(BF16), 'w_vc': w_vc.astype(BF16), 'w_qb': qb.astype(BF16),
        'ones': jnp.broadcast_to(ones[None], (depth,) + ones.shape),
        'gk': jnp.tile(a_k_gain, (1, heads_per_tile))[:, None, :],
        'gq': jnp.tile(a_q_gain, (1, heads_per_tile))[:, None, :] * (A_HEAD_DIM ** -0.5 * LOG2_E),
        'gkv': c_kv_a_gain[:, None, :], 'gcq': c_q_a_gain[:, None, :],
    }


def kernel(x, c, ctx, c_ctx, w_mod, b_mod, w_in, a_q_gain, a_k_gain, c_q_a_gain, c_kv_a_gain, c_w_qb, c_w_kvb, s5_a_re, s5_a_im, s5_log_dt, s5_b_re, s5_b_im, s5_c_re, s5_c_im, s5_d, s5_w_glu, w_branch_a, w_branch_s5, w_branch_c, w_out, ln1_g, ln1_b, w_up, w_down, ln2_g, ln2_b):
    bsz, seq, _ = x.shape
    tabs_lat = _rope_lane_tables(seq, A_HEAD_DIM, 0) + _rope_lane_tables(seq, C_ROPE, C_NOPE)
    c_all = jnp.zeros((8, D_MODEL), F32).at[:bsz].set(c).at[bsz].set(c_ctx)
    zero_carry = jnp.zeros((bsz, 2, S5_LANES), F32)

    lw = _layer_weights(w_in, a_q_gain, a_k_gain, c_q_a_gain, c_kv_a_gain, c_w_qb, c_w_kvb)
    sw = _s5_weights(s5_a_re, s5_a_im, s5_log_dt, s5_b_re, s5_b_im, s5_c_re, s5_c_im)
    d_rows = s5_d[:, None, :]
    w_glu = s5_w_glu.astype(BF16)
    merge_w = [lw['w_gate'], w_branch_a.astype(BF16), w_branch_s5.astype(BF16), w_branch_c.astype(BF16),
               w_out.astype(BF16), ln1_g[:, None, :], ln1_b[:, None, :]]
    mlp_w = [w_up.astype(BF16), w_down.astype(BF16), ln2_g[:, None, :], ln2_b[:, None, :]]

    for l in range(DEPTH):
        last = l == DEPTH - 1
        mod = _modulation(c_all, w_mod, b_mod, l).reshape(8, N_MOD, D_MODEL)
        mod_lat, mod_ctx = mod[:bsz], mod[bsz:bsz + 1]

        ka_c, va_c, kc_c, vc_c, u_c, qa_c, qc_c = _inproj(ctx, mod_ctx, lw, l, tabs_lat, rope=False)
        ka, va, kc, vc, u, qa, qc = _inproj(x, mod_lat, lw, l, tabs_lat, rope=True)

        u_c, u = _interleave(u_c), _interleave(u)
        yf_c, carry_f = _s5_scan(u_c, zero_carry, sw, l, u_c, d_rows, w_glu, reverse=False, glu=False)
        yf, _ = _s5_scan(u, carry_f, sw, l, u, d_rows, w_glu, reverse=False, glu=False)
        ys_c, carry_b = _s5_scan(u_c, zero_carry, sw, l, yf_c, d_rows, w_glu, reverse=True, glu=True)
        ys, _ = _s5_scan(u, carry_b, sw, l, yf, d_rows, w_glu, reverse=True, glu=True)
        ys_c, ys = _deinterleave(ys_c), _deinterleave(ys)

        ya = _attention(qa, ka, ka_c, va, va_c, heads=A_HEADS, kv_heads=A_KV_HEADS,
                        dk=A_HEAD_DIM, dv=A_HEAD_DIM, name="attn_a")
        yc = _attention(qc, kc, kc_c, vc, vc_c, heads=C_HEADS, kv_heads=C_HEADS,
                        dk=C_HEAD_PAD, dv=C_VDIM, name="attn_c")
        x_mid = _token_call(_merge_kernel, "merge", x, mod_lat, [ya, ys, yc], merge_w, l, MERGE_ROW_BLOCK)
        x_next = _token_call(_mlp_kernel, "mlp", x_mid, mod_lat, [], mlp_w, l, MLP_ROW_BLOCK)

        if not last:
            ya_c = _attention_ctx(qa_c, ka_c, va_c, heads=A_HEADS, kv_heads=A_KV_HEADS,
                                  dk=A_HEAD_DIM, dv=A_HEAD_DIM, name="attn_a_ctx")
            yc_c = _attention_ctx(qc_c, kc_c, vc_c, heads=C_HEADS, kv_heads=C_HEADS,
                                  dk=C_HEAD_PAD, dv=C_VDIM, name="attn_c_ctx")
            ctx_mid = _token_call(_merge_kernel, "merge_ctx", ctx, mod_ctx, [ya_c, ys_c, yc_c], merge_w, l,
                                  MERGE_ROW_BLOCK)
            ctx = _token_call(_mlp_kernel, "mlp_ctx", ctx_mid, mod_ctx, [], mlp_w, l, MLP_ROW_BLOCK)
        x = x_next
    return x
```

```python
import functools
import math

import jax
import jax.numpy as jnp
from jax import lax
from jax.experimental import pallas as pl
from jax.experimental.pallas import tpu as pltpu

F32 = jnp.float32
BF16 = jnp.bfloat16

D_MODEL = 1024
DEPTH = 2
GRID_W = 64
ROPE_THETA = 10000.0
EPS = 1e-6

A_HEADS = 8
A_KV_HEADS = 2
A_HEAD_DIM = 64
A_WIDTH = A_HEADS * A_HEAD_DIM

S5_CH = 512
S5_GROUP_CH = 16
S5_GROUPS = S5_CH // S5_GROUP_CH
S5_STATE = 64
S5_LANES = S5_GROUPS * S5_STATE

C_HEADS = 8
C_NOPE = 64
C_ROPE = 32
C_VDIM = 64
C_Q_RANK = 768
C_KV_RANK = 256
C_QK_DIM = C_NOPE + C_ROPE

N_BRANCH = 3
N_MOD = 6
DEEPNORM_ALPHA = (2.0 * DEPTH) ** 0.25
LOG2_E = math.log2(math.e)

OFF_AK = 0
OFF_AV = OFF_AK + A_KV_HEADS * A_HEAD_DIM
OFF_CKV = OFF_AV + A_KV_HEADS * A_HEAD_DIM
OFF_CKR = OFF_CKV + C_KV_RANK
OFF_U = OFF_CKR + C_ROPE
OFF_AQ = OFF_U + S5_CH
OFF_CQ = OFF_AQ + A_WIDTH
OFF_GATE = OFF_CQ + C_Q_RANK

LANE = 128
C_HEAD_PAD = LANE

P_AK = 0
P_AV = P_AK + LANE
P_CKV = P_AV + LANE
P_CKR = P_CKV + C_KV_RANK
P_U = P_CKR + LANE
P_AQ = P_U + S5_CH
P_CQ = P_AQ + A_WIDTH
P_GATE = P_CQ + C_Q_RANK

MLP_ROW_BLOCK = 512
MERGE_ROW_BLOCK = 1024
INPROJ_ROW_BLOCK = 1024
ATTN_TQ = 256
ATTN_KEY_CHUNK = 1024
S5_CHUNK = 256
S5_SUB = 8
S5_SUB_LEN = S5_CHUNK // S5_SUB
S5_LANE_CHUNK = 512
VMEM_LIMIT = 56 * 1024 * 1024


def _sel_spec(arr, idx):
    nd = arr.ndim - len(idx)
    return pl.BlockSpec((None,) * len(idx) + tuple(arr.shape[len(idx):]), lambda *_: tuple(idx) + (0,) * nd,
                        pipeline_mode=pl.Buffered(1))


def _params(n_grid):
    return pltpu.CompilerParams(dimension_semantics=("arbitrary",) * n_grid,
                                vmem_limit_bytes=VMEM_LIMIT)


def _layer_norm(x):
    mu = jnp.mean(x, axis=-1, keepdims=True)
    xc = x - mu
    var = jnp.mean(xc * xc, axis=-1, keepdims=True)
    return xc * lax.rsqrt(var + EPS)


def _dot(a, b):
    return jnp.dot(a, b, preferred_element_type=F32)


def _mod_kernel(c_ref, w_ref, b_ref, o_ref):
    s = jax.nn.silu(c_ref[...]).astype(BF16)
    o_ref[...] = _dot(s, w_ref[...].astype(BF16)) + b_ref[...]


def _modulation(c_all, w_mod, b_mod, layer):
    n = N_MOD * D_MODEL
    bn = n // 4
    return pl.pallas_call(
        _mod_kernel,
        grid=(n // bn,),
        in_specs=[pl.BlockSpec((8, D_MODEL), lambda j: (0, 0)),
                  pl.BlockSpec((None, D_MODEL, bn), lambda j: (layer, 0, j)),
                  pl.BlockSpec((None, 1, bn), lambda j: (layer, 0, j))],
        out_specs=pl.BlockSpec((8, bn), lambda j: (0, j)),
        out_shape=jax.ShapeDtypeStruct((8, n), F32),
        compiler_params=_params(1),
        name="mod",
    )(c_all, w_mod, b_mod.reshape(DEPTH, 1, n))


def _rope(x, cos, sin_next, sin_prev, shift):
    n = x.shape[-1]
    return x * cos + pltpu.roll(x, n - shift, 1) * sin_next + pltpu.roll(x, shift, 1) * sin_prev


def _head_rms_norm(p, ones_ref, gain):
    sq = p * p
    hi = sq.astype(BF16)
    lo = (sq - hi.astype(F32)).astype(BF16)
    ms = _dot(jnp.concatenate([hi, lo], axis=1), ones_ref[...])
    return p * lax.rsqrt(ms + EPS) * gain


def _row_rms_norm(p, gain):
    return p * lax.rsqrt(jnp.mean(p * p, axis=-1, keepdims=True) + EPS) * gain


def _value_tile(v, head):
    half = LANE // 2
    t = v[:, (head // 2) * LANE:(head // 2 + 1) * LANE]
    if head % 2:
        t = pltpu.roll(t, half, 1)
    low = lax.broadcasted_iota(jnp.int32, t.shape, 1) < half
    return jnp.where(low, t, 1.0).astype(BF16)


def _inproj_kernel(x_ref, mod_ref, w_ref, wkc_ref, wvc_ref, wqb_ref, ones_ref,
                   gk_ref, gq_ref, gkv_ref, gcq_ref,
                   cosa_ref, sna_ref, spa_ref, cosc_ref, snc_ref, spc_ref,
                   ka_ref, va_ref, kc_ref, vc_ref, u_ref, qa_ref, qc_ref, *, rope):
    mod = mod_ref[0]
    h = (_layer_norm(x_ref[0]) * (1.0 + mod[1:2]) + mod[0:1]).astype(BF16)

    def proj(a, b):
        return _dot(h, w_ref[:, a:b])

    def rope_a(t):
        return _rope(t, cosa_ref[...], sna_ref[...], spa_ref[...], A_HEAD_DIM // 4) if rope else t

    def rope_c(t):
        return _rope(t, cosc_ref[...], snc_ref[...], spc_ref[...], C_ROPE // 4) if rope else t

    state = proj(P_AK, P_U)
    kt = rope_a(_head_rms_norm(state[:, P_AK:P_AV], ones_ref, gk_ref[...])).T.astype(BF16)
    v = state[:, P_AV:P_CKV]
    for hh in range(A_KV_HEADS):
        ka_ref[0, hh] = kt[hh * A_HEAD_DIM:(hh + 1) * A_HEAD_DIM, :]
        va_ref[0, hh] = _value_tile(v, hh)

    ckv = _row_rms_norm(state[:, P_CKV:P_CKR], gkv_ref[...]).astype(BF16)
    k_rope = rope_c(state[:, P_CKR:P_U])
    k_nope = _dot(ckv, wkc_ref[...])
    vc = _dot(ckv, wvc_ref[...])
    for hh in range(C_HEADS):
        kc_ref[0, hh] = (k_nope[:, hh * C_HEAD_PAD:(hh + 1) * C_HEAD_PAD] + k_rope).T.astype(BF16)
        vc_ref[0, hh] = _value_tile(vc, hh)

    u_ref[0] = proj(P_U, P_AQ)

    aq = proj(P_AQ, P_CQ)
    for t in range(A_WIDTH // LANE):
        q = _head_rms_norm(aq[:, t * LANE:(t + 1) * LANE], ones_ref, gq_ref[...])
        qa_ref[0, :, t * LANE:(t + 1) * LANE] = rope_a(q).astype(BF16)

    cq = _row_rms_norm(proj(P_CQ, P_GATE), gcq_ref[...]).astype(BF16)
    qc = _dot(cq, wqb_ref[...])
    for hh in range(C_HEADS):
        q = rope_c(qc[:, hh * C_HEAD_PAD:(hh + 1) * C_HEAD_PAD])
        qc_ref[0, :, hh * C_HEAD_PAD:(hh + 1) * C_HEAD_PAD] = (q * (C_QK_DIM ** -0.5 * LOG2_E)).astype(BF16)


def _inproj(x, mod, lw, layer, tabs, rope):
    bsz, n_tok, _ = x.shape
    rb = min(INPROJ_ROW_BLOCK, n_tok)
    nb = n_tok // rb
    per_batch_mod = mod.shape[0] > 1

    def row_spec(width):
        return pl.BlockSpec((1, rb, width), lambda b, i: (b, i, 0))

    def head_spec(heads, width):
        return pl.BlockSpec((1, heads, rb, width), lambda b, i: (b, 0, i, 0))

    tab_spec = pl.BlockSpec((rb, LANE), lambda b, i: (i, 0))
    mod_spec = pl.BlockSpec((1, N_MOD, D_MODEL), (lambda b, i: (b, 0, 0)) if per_batch_mod else (lambda b, i: (0, 0, 0)))
    consts = [lw['w_in'], lw['w_kc'], lw['w_vc'], lw['w_qb'], lw['ones'],
              lw['gk'], lw['gq'], lw['gkv'], lw['gcq']]
    def head_t_spec(heads, width):
        return pl.BlockSpec((1, heads, width, rb), lambda b, i: (b, 0, 0, i))

    out_shape = [
        jax.ShapeDtypeStruct((bsz, A_KV_HEADS, A_HEAD_DIM, n_tok), BF16),
        jax.ShapeDtypeStruct((bsz, A_KV_HEADS, n_tok, 2 * A_HEAD_DIM), BF16),
        jax.ShapeDtypeStruct((bsz, C_HEADS, C_HEAD_PAD, n_tok), BF16),
        jax.ShapeDtypeStruct((bsz, C_HEADS, n_tok, 2 * C_VDIM), BF16),
        jax.ShapeDtypeStruct((bsz, n_tok, S5_CH), F32),
        jax.ShapeDtypeStruct((bsz, n_tok, A_WIDTH), BF16),
        jax.ShapeDtypeStruct((bsz, n_tok, C_HEADS * C_HEAD_PAD), BF16),
    ]
    out_specs = [head_t_spec(A_KV_HEADS, A_HEAD_DIM), head_spec(A_KV_HEADS, 2 * A_HEAD_DIM),
                 head_t_spec(C_HEADS, C_HEAD_PAD), head_spec(C_HEADS, 2 * C_VDIM),
                 row_spec(S5_CH), row_spec(A_WIDTH), row_spec(C_HEADS * C_HEAD_PAD)]
    return pl.pallas_call(
        functools.partial(_inproj_kernel, rope=rope),
        grid=(bsz, nb),
        in_specs=[row_spec(D_MODEL), mod_spec] + [_sel_spec(a, (layer,)) for a in consts] + [tab_spec] * 6,
        out_specs=out_specs,
        out_shape=out_shape,
        compiler_params=_params(2),
        name="inproj",
    )(x, mod, *consts, *tabs)


def _scores(q_ref, q_rows, kt_refs, s_ref, m_ref, *, shared_kv, dk):
    tq = ATTN_TQ
    q = q_ref[0, q_rows, :]
    if shared_kv:
        lhs = [(slice(0, 2 * tq), jnp.concatenate([q[:, :dk], q[:, dk:]], axis=0), 0)]
    else:
        lhs = [(slice(hh * tq, (hh + 1) * tq), q[:, hh * dk:(hh + 1) * dk], hh) for hh in range(2)]
    for rows, qh, hh in lhs:
        m = None
        off = 0
        for kt_ref in kt_refs:
            n = kt_ref.shape[-1]
            for c0 in range(0, n, ATTN_KEY_CHUNK):
                c1 = min(c0 + ATTN_KEY_CHUNK, n)
                s = _dot(qh, kt_ref[0, hh, :, c0:c1])
                s_ref[rows, off + c0:off + c1] = s
                mc = s.max(axis=-1, keepdims=True)
                m = mc if m is None else jnp.maximum(m, mc)
            off += n
        m_ref[rows] = m


def _exp_scores(s_ref, m_ref, p_ref):
    m = m_ref[...]
    n = s_ref.shape[-1]
    for c0 in range(0, n, ATTN_KEY_CHUNK):
        c1 = min(c0 + ATTN_KEY_CHUNK, n)
        p_ref[:, c0:c1] = jnp.exp2(s_ref[:, c0:c1] - m).astype(BF16)


def _weighted_values(p_ref, v_refs, o_ref, o_rows, *, shared_kv, dv):
    tq = ATTN_TQ
    for hh in range(2):
        rows = slice(hh * tq, (hh + 1) * tq)
        o = None
        off = 0
        for v_ref in v_refs:
            n = v_ref.shape[2]
            part = _dot(p_ref[rows, off:off + n], v_ref[0, 0 if shared_kv else hh])
            o = part if o is None else o + part
            off += n
        o_ref[0, o_rows, hh * dv:(hh + 1) * dv] = (o[:, :dv] / o[:, dv:]).astype(BF16)


def _attn_pipe_kernel(run_ref, q_ref, kt_lat, kt_ctx, v_lat, v_ctx, o_ref,
                      s_a, s_b, p_a, p_b, m_a, m_b, *, shared_kv, dk, dv):
    g = pl.program_id(0)

    @pl.when(g == 0)
    def _():
        s_b[...] = jnp.zeros_like(s_b)
        m_b[...] = jnp.zeros_like(m_b)
        p_a[...] = jnp.ones_like(p_a)

    kts, vs = [kt_lat, kt_ctx], [v_lat, v_ctx]

    def tick(rows, s_new, m_new, s_old, m_old, p_new, p_old):
        _scores(q_ref, rows, kts, s_new, m_new, shared_kv=shared_kv, dk=dk)
        _exp_scores(s_old, m_old, p_new)
        _weighted_values(p_old, vs, o_ref, rows, shared_kv=shared_kv, dv=dv)

    @pl.when(run_ref[0] == 1)
    def _():
        tick(slice(0, ATTN_TQ), s_a, m_a, s_b, m_b, p_b, p_a)

    @pl.when(run_ref[1] == 1)
    def _():
        tick(slice(ATTN_TQ, 2 * ATTN_TQ), s_b, m_b, s_a, m_a, p_a, p_b)


def _attn_ctx_kernel(q_ref, kt_ref, v_ref, o_ref, *, shared_kv, dk, dv):
    for hh in range(2):
        kv = 0 if shared_kv else hh
        s = _dot(q_ref[0, :, hh * dk:(hh + 1) * dk], kt_ref[0, kv])
        p = jnp.exp2(s - s.max(axis=-1, keepdims=True)).astype(BF16)
        o = _dot(p, v_ref[0, kv])
        o_ref[0, :, hh * dv:(hh + 1) * dv] = (o[:, :dv] / o[:, dv:]).astype(BF16)


def _attention(q, kt_lat, kt_ctx, v_lat, v_ctx, *, heads, kv_heads, dk, dv, name):
    bsz, lq, _ = q.shape
    shared_kv = kv_heads < heads
    group = heads // kv_heads
    n_pairs = heads // 2
    n_qb = lq // (2 * ATTN_TQ)
    n_steps = bsz * n_pairs * n_qb
    lag = 1
    kv_blk = 1 if shared_kv else 2
    n_lat, n_ctx = kt_lat.shape[-1], kt_ctx.shape[-1]

    def split(g):
        return g // (n_pairs * n_qb), (g // n_qb) % n_pairs, g % n_qb

    def q_map(g):
        b, p, i = split(jnp.minimum(g, n_steps - 1))
        return b, i, p

    def k_map(g):
        b, p, _ = split(jnp.minimum(g, n_steps - 1))
        return b, (2 * p) // group if shared_kv else p, 0, 0

    def v_map(g):
        b, p, _ = split(jnp.maximum(g - lag, 0))
        return b, (2 * p) // group if shared_kv else p, 0, 0

    def o_map(g):
        b, p, i = split(jnp.maximum(g - lag, 0))
        return b, i, p

    rows, keys = 2 * ATTN_TQ, n_lat + n_ctx
    return pl.pallas_call(
        functools.partial(_attn_pipe_kernel, shared_kv=shared_kv, dk=dk, dv=dv),
        grid=(n_steps + lag,),
        in_specs=[pl.BlockSpec(memory_space=pltpu.SMEM),
                  pl.BlockSpec((1, 2 * ATTN_TQ, 2 * dk), q_map),
                  pl.BlockSpec((1, kv_blk, dk, n_lat), k_map), pl.BlockSpec((1, kv_blk, dk, n_ctx), k_map),
                  pl.BlockSpec((1, kv_blk, n_lat, 2 * dv), v_map), pl.BlockSpec((1, kv_blk, n_ctx, 2 * dv), v_map)],
        out_specs=pl.BlockSpec((1, 2 * ATTN_TQ, 2 * dv), o_map),
        out_shape=jax.ShapeDtypeStruct((bsz, lq, heads * dv), BF16),
        scratch_shapes=[pltpu.VMEM((rows, keys), F32), pltpu.VMEM((rows, keys), F32),
                        pltpu.VMEM((rows, keys), BF16), pltpu.VMEM((rows, keys), BF16)]
                       + [pltpu.VMEM((rows, 1), F32)] * 2,
        compiler_params=_params(1),
        name=name,
    )(jnp.ones((2,), jnp.int32), q, kt_lat, kt_ctx, v_lat, v_ctx)


def _attention_ctx(q, kt, v, *, heads, kv_heads, dk, dv, name):
    bsz, lq, _ = q.shape
    shared_kv = kv_heads < heads
    group = heads // kv_heads
    kv_blk = 1 if shared_kv else 2
    kmap = (lambda b, p: (b, (2 * p) // group, 0, 0)) if shared_kv else (lambda b, p: (b, p, 0, 0))
    return pl.pallas_call(
        functools.partial(_attn_ctx_kernel, shared_kv=shared_kv, dk=dk, dv=dv),
        grid=(bsz, heads // 2),
        in_specs=[pl.BlockSpec((1, lq, 2 * dk), lambda b, p: (b, 0, p)),
                  pl.BlockSpec((1, kv_blk, dk, kt.shape[-1]), kmap),
                  pl.BlockSpec((1, kv_blk, v.shape[2], 2 * dv), kmap)],
        out_specs=pl.BlockSpec((1, lq, 2 * dv), lambda b, p: (b, 0, p)),
        out_shape=jax.ShapeDtypeStruct((bsz, lq, heads * dv), BF16),
        compiler_params=_params(2),
        name=name,
    )(q, kt, v)


def _s5_kernel(u_ref, cin_ref, wb_ref, a_ref, apow_ref, wc_ref, yprev_ref, d_ref, wglu_ref,
               out_ref, cout_ref, bu_r, bu_i, xb, car, *, reverse, glu):
    @pl.when(pl.program_id(1) == 0)
    def _():
        car[...] = cin_ref[0]

    ub = u_ref[0].astype(BF16)
    n_tiles = S5_CH // LANE
    tile_states = S5_LANES // n_tiles
    for t in range(n_tiles):
        r = _dot(ub[:, t * LANE:(t + 1) * LANE], wb_ref[t])
        bu_r[:, t * tile_states:(t + 1) * tile_states] = r[:, :tile_states]
        bu_i[:, t * tile_states:(t + 1) * tile_states] = r[:, tile_states:]

    steps = range(S5_SUB_LEN - 1, -1, -1) if reverse else range(S5_SUB_LEN)
    subs = range(S5_SUB - 1, -1, -1) if reverse else range(S5_SUB)
    pack_rows = 2 * S5_SUB
    for c in range(S5_LANES // S5_LANE_CHUNK):
        ls = slice(c * S5_LANE_CHUNK, (c + 1) * S5_LANE_CHUNK)
        ar = jnp.broadcast_to(a_ref[0:1, ls], (S5_SUB, S5_LANE_CHUNK))
        ai = jnp.broadcast_to(a_ref[1:2, ls], (S5_SUB, S5_LANE_CHUNK))
        xr = jnp.zeros((S5_SUB, S5_LANE_CHUNK), F32)
        xi = jnp.zeros((S5_SUB, S5_LANE_CHUNK), F32)
        for i in steps:
            rows = slice(i * S5_SUB, (i + 1) * S5_SUB)
            xr, xi = (ar * xr - ai * xi + bu_r[rows, ls], ar * xi + ai * xr + bu_i[rows, ls])
            bu_r[rows, ls] = xr
            bu_i[rows, ls] = xi
        cr, ci = car[0:1, ls], car[1:2, ls]
        a_sub_r, a_sub_i = a_ref[2:3, ls], a_ref[3:4, ls]
        crs, cis = [None] * S5_SUB, [None] * S5_SUB
        for j in subs:
            crs[j], cis[j] = cr, ci
            cr, ci = (a_sub_r * cr - a_sub_i * ci + xr[j:j + 1], a_sub_r * ci + a_sub_i * cr + xi[j:j + 1])
        car[0:1, ls] = cr
        car[1:2, ls] = ci
        cmr = jnp.concatenate(crs * (pack_rows // S5_SUB), axis=0)
        cmi = jnp.concatenate(cis * (pack_rows // S5_SUB), axis=0)
        col = (c * S5_LANE_CHUNK // tile_states) * 2 * tile_states + (c * S5_LANE_CHUNK) % tile_states
        for g in range(S5_CHUNK // pack_rows):
            rows = slice(g * pack_rows, (g + 1) * pack_rows)
            pr, pi = apow_ref[0, rows, ls], apow_ref[1, rows, ls]
            xb[rows, col:col + S5_LANE_CHUNK] = (bu_r[rows, ls] + (pr * cmr - pi * cmi)).astype(BF16)
            xb[rows, col + tile_states:col + tile_states + S5_LANE_CHUNK] = (
                bu_i[rows, ls] + (pr * cmi + pi * cmr)).astype(BF16)
    cout_ref[0] = car[...]

    ys = [_dot(xb[:, t * 2 * tile_states:(t + 1) * 2 * tile_states], wc_ref[t]) for t in range(n_tiles)]
    y = jnp.concatenate(ys, axis=-1)
    if glu:
        y = y + yprev_ref[0] + d_ref[...] * u_ref[0]
        hg = _dot(jax.nn.gelu(y).astype(BF16), wglu_ref[...])
        out_ref[0] = (hg[:, :S5_CH] * jax.nn.sigmoid(hg[:, S5_CH:])).astype(out_ref.dtype)
    else:
        out_ref[0] = y


def _s5_scan(u, carry_in, sw, layer, yprev, d, w_glu, *, reverse, glu):
    bsz, n_tok, _ = u.shape
    nc = n_tok // S5_CHUNK
    order = (lambda b, i: (b, nc - 1 - i, 0)) if reverse else (lambda b, i: (b, i, 0))
    tile_states = S5_LANES // (S5_CH // LANE)
    assert tile_states % S5_LANE_CHUNK == 0
    row_spec = pl.BlockSpec((1, S5_CHUNK, S5_CH), order)
    carry_spec = pl.BlockSpec((1, 2, S5_LANES), lambda b, i: (b, 0, 0))
    consts = [sw['wb'], sw['a'], sw['apow'], sw['wc']]
    out, carry = pl.pallas_call(
        functools.partial(_s5_kernel, reverse=reverse, glu=glu),
        grid=(bsz, nc),
        in_specs=[row_spec, carry_spec] + [_sel_spec(a, (layer, int(reverse))) for a in consts]
                 + [row_spec, _sel_spec(d, (layer,)), _sel_spec(w_glu, (layer,))],
        out_specs=[row_spec, carry_spec],
        out_shape=[jax.ShapeDtypeStruct((bsz, n_tok, S5_CH), BF16 if glu else F32),
                   jax.ShapeDtypeStruct((bsz, 2, S5_LANES), F32)],
        scratch_shapes=[pltpu.VMEM((S5_CHUNK, S5_LANES), F32), pltpu.VMEM((S5_CHUNK, S5_LANES), F32),
                        pltpu.VMEM((S5_CHUNK, 2 * S5_LANES), BF16), pltpu.VMEM((2, S5_LANES), F32)],
        compiler_params=_params(2),
        name="s5_bwd" if reverse else "s5_fwd",
    )(u, carry_in, *consts, yprev, d, w_glu)
    return out, carry


def _merge_kernel(x_ref, mod_ref, ya_ref, ys_ref, yc_ref, wg_ref, wa_ref, ws_ref, wc_ref, wo_ref,
                  g_ref, b_ref, o_ref):
    x = x_ref[0]
    mod = mod_ref[0]
    h = (_layer_norm(x) * (1.0 + mod[1:2]) + mod[0:1]).astype(BF16)
    merged = None
    for t, (y_ref, w_ref) in enumerate(((ya_ref, wa_ref), (ys_ref, ws_ref), (yc_ref, wc_ref))):
        gate = jax.nn.sigmoid(_dot(h, wg_ref[:, t * D_MODEL:(t + 1) * D_MODEL]))
        term = gate * _dot(y_ref[0], w_ref[...])
        merged = term if merged is None else merged + term
    mix = _dot(merged.astype(BF16), wo_ref[...])
    y = DEEPNORM_ALPHA * x + mod[2:3] * mix
    o_ref[0] = _layer_norm(y) * g_ref[...] + b_ref[...]


def _mlp_kernel(x_ref, mod_ref, wu_ref, wd_ref, g_ref, b_ref, o_ref):
    x = x_ref[0]
    mod = mod_ref[0]
    h = (_layer_norm(x) * (1.0 + mod[4:5]) + mod[3:4]).astype(BF16)
    up = jnp.square(jnp.maximum(_dot(h, wu_ref[...]), 0.0)).astype(BF16)
    y = DEEPNORM_ALPHA * x + mod[5:6] * _dot(up, wd_ref[...])
    o_ref[0] = _layer_norm(y) * g_ref[...] + b_ref[...]


def _token_call(kernel, name, x, mod, rows, consts, layer, row_block):
    bsz, n_tok, _ = x.shape
    per_batch_mod = mod.shape[0] > 1
    rb = min(row_block, n_tok)

    def row_spec(width):
        return pl.BlockSpec((1, rb, width), lambda b, i: (b, i, 0))

    mod_spec = pl.BlockSpec((1, N_MOD, D_MODEL), (lambda b, i: (b, 0, 0)) if per_batch_mod else (lambda b, i: (0, 0, 0)))
    return pl.pallas_call(
        kernel,
        grid=(bsz, n_tok // rb),
        in_specs=[row_spec(D_MODEL), mod_spec] + [row_spec(r.shape[-1]) for r in rows]
                 + [_sel_spec(a, (layer,)) for a in consts],
        out_specs=row_spec(D_MODEL),
        out_shape=jax.ShapeDtypeStruct((bsz, n_tok, D_MODEL), F32),
        compiler_params=_params(2),
        name=name,
    )(x, mod, *rows, *consts)


def _axial_rope_tables(rows, dim):
    half = dim // 2
    inv = ROPE_THETA ** (-jnp.arange(0, half, 2, dtype=F32) / half)
    row = jnp.repeat(jnp.arange(rows, dtype=F32), GRID_W)
    col = jnp.tile(jnp.arange(GRID_W, dtype=F32), rows)
    ang_r = row[:, None] * inv
    ang_c = col[:, None] * inv
    ang = jnp.concatenate([ang_r, ang_r, ang_c, ang_c], axis=-1)
    return jnp.cos(ang), jnp.sin(ang)


def _rope_lane_tables(n_tok, dim, lane_off):
    cos, sin = _axial_rope_tables(n_tok // GRID_W, dim)
    reps = (LANE - lane_off) // dim if lane_off == 0 else 1
    cos_t = jnp.ones((n_tok, LANE), F32).at[:, lane_off:lane_off + reps * dim].set(jnp.tile(cos, (1, reps)))
    sin_t = jnp.zeros((n_tok, LANE), F32).at[:, lane_off:lane_off + reps * dim].set(jnp.tile(sin, (1, reps)))
    quarter = dim // 4
    first = (jnp.arange(LANE) % (2 * quarter)) < quarter
    return cos_t, jnp.where(first, -sin_t, 0.0), jnp.where(first, 0.0, sin_t)


def _block_diag(blocks):
    n, r, c = blocks.shape
    eye = jnp.eye(n, dtype=blocks.dtype)
    return (eye[:, None, :, None] * blocks[:, :, None, :]).reshape(n * r, n * c)


def _s5_discretize(a_re, a_im, log_dt, b_re, b_im):
    dt = jnp.exp(log_dt)[:, None]
    mag = jnp.exp(a_re * dt)
    abar_r = mag * jnp.cos(a_im * dt)
    abar_i = mag * jnp.sin(a_im * dt)
    den = a_re * a_re + a_im * a_im
    nr = abar_r - 1.0
    coef_r = (nr * a_re + abar_i * a_im) / den
    coef_i = (abar_i * a_re - nr * a_im) / den
    bbar_r = coef_r[..., None] * b_re - coef_i[..., None] * b_im
    bbar_i = coef_r[..., None] * b_im + coef_i[..., None] * b_re
    return abar_r, abar_i, bbar_r, bbar_i


def _s5_direction_weights(a_re, a_im, log_dt, b_re, b_im, c_re, c_im):
    abar_r, abar_i, bbar_r, bbar_i = _s5_discretize(a_re, a_im, log_dt, b_re, b_im)
    groups_per_tile = LANE // S5_GROUP_CH
    n_tiles = S5_GROUPS // groups_per_tile

    def in_tile(bbar):
        blk = jnp.swapaxes(bbar, 1, 2).reshape(n_tiles, groups_per_tile, S5_GROUP_CH, S5_STATE)
        return jax.vmap(_block_diag)(blk)

    def out_tile(cm):
        blk = jnp.swapaxes(cm, 1, 2).reshape(n_tiles, groups_per_tile, S5_STATE, S5_GROUP_CH)
        return jax.vmap(_block_diag)(blk)

    wb = jnp.concatenate([in_tile(bbar_r), in_tile(bbar_i)], axis=-1).astype(BF16)
    wc = jnp.concatenate([out_tile(c_re), -out_tile(c_im)], axis=1).astype(BF16)

    def step(carry, _):
        pr, pi = carry
        nxt = (pr * abar_r - pi * abar_i, pr * abar_i + pi * abar_r)
        return nxt, nxt
    _, (pows_r, pows_i) = lax.scan(step, (jnp.ones_like(abar_r), jnp.zeros_like(abar_r)), None, length=S5_SUB_LEN)
    pows = jnp.stack([pows_r.reshape(S5_SUB_LEN, S5_LANES), pows_i.reshape(S5_SUB_LEN, S5_LANES)])
    return wb, wc, jnp.stack([abar_r.reshape(-1), abar_i.reshape(-1)]), pows


def _s5_weights(a_re, a_im, log_dt, b_re, b_im, c_re, c_im):
    wb, wc, abar, pows = jax.vmap(jax.vmap(_s5_direction_weights))(a_re, a_im, log_dt, b_re, b_im, c_re, c_im)
    sub = pows[:, :, :, -1]
    pows = jnp.stack([pows[:, 0], pows[:, 1, :, ::-1]], axis=1)
    a = jnp.concatenate([abar, sub], axis=2)
    apow = jnp.repeat(pows, S5_SUB, axis=3)
    return {'wb': wb, 'a': a, 'apow': apow, 'wc': wc}


def _interleave(t):
    b, n, w = t.shape
    return t.reshape(b, n // S5_CHUNK, S5_SUB, S5_SUB_LEN, w).swapaxes(2, 3).reshape(b, n, w)


def _deinterleave(t):
    b, n, w = t.shape
    return t.reshape(b, n // S5_CHUNK, S5_SUB_LEN, S5_SUB, w).swapaxes(2, 3).reshape(b, n, w)


def _layer_weights(w_in, a_q_gain, a_k_gain, c_q_a_gain, c_kv_a_gain, c_w_qb, c_w_kvb):
    depth = w_in.shape[0]
    ckr = jnp.zeros((depth, D_MODEL, LANE), F32).at[:, :, C_NOPE:C_QK_DIM].set(w_in[:, :, OFF_CKR:OFF_U])
    w_re = jnp.concatenate([w_in[:, :, OFF_AK:OFF_CKR], ckr, w_in[:, :, OFF_U:OFF_GATE]], axis=2).astype(BF16)
    assert w_re.shape[2] == P_GATE
    no_pad = ((0, 0), (0, 0), (0, 0))
    qb = c_w_qb.reshape(depth, C_Q_RANK, C_HEADS, C_QK_DIM)
    qb = jnp.pad(qb, no_pad + ((0, C_HEAD_PAD - C_QK_DIM),)).reshape(depth, C_Q_RANK, C_HEADS * C_HEAD_PAD)
    kvb = c_w_kvb.reshape(depth, C_KV_RANK, C_HEADS, C_NOPE + C_VDIM)
    w_kc = jnp.pad(kvb[..., :C_NOPE], no_pad + ((0, C_HEAD_PAD - C_NOPE),))
    w_kc = w_kc.reshape(depth, C_KV_RANK, C_HEADS * C_HEAD_PAD)
    w_vc = kvb[..., C_NOPE:].reshape(depth, C_KV_RANK, C_HEADS * C_VDIM)
    heads_per_tile = LANE // A_HEAD_DIM
    ones = _block_diag(jnp.full((heads_per_tile, A_HEAD_DIM, A_HEAD_DIM), 1.0 / A_HEAD_DIM, F32))
    ones = jnp.concatenate([ones, ones], axis=0).astype(BF16)
    return {
        'w_in': w_re, 'w_gate': w_in[:, :, OFF_GATE:].astype(BF16),
        'w_kc': w_kc.astype(BF16), 'w_vc': w_vc.astype(BF16), 'w_qb': qb.astype(BF16),
        'ones': jnp.broadcast_to(ones[None], (depth,) + ones.shape),
        'gk': jnp.tile(a_k_gain, (1, heads_per_tile))[:, None, :],
        'gq': jnp.tile(a_q_gain, (1, heads_per_tile))[:, None, :] * (A_HEAD_DIM ** -0.5 * LOG2_E),
        'gkv': c_kv_a_gain[:, None, :], 'gcq': c_q_a_gain[:, None, :],
    }


def kernel(x, c, ctx, c_ctx, w_mod, b_mod, w_in, a_q_gain, a_k_gain, c_q_a_gain, c_kv_a_gain, c_w_qb, c_w_kvb, s5_a_re, s5_a_im, s5_log_dt, s5_b_re, s5_b_im, s5_c_re, s5_c_im, s5_d, s5_w_glu, w_branch_a, w_branch_s5, w_branch_c, w_out, ln1_g, ln1_b, w_up, w_down, ln2_g, ln2_b):
    bsz, seq, _ = x.shape
    tabs_lat = _rope_lane_tables(seq, A_HEAD_DIM, 0) + _rope_lane_tables(seq, C_ROPE, C_NOPE)
    c_all = jnp.zeros((8, D_MODEL), F32).at[:bsz].set(c).at[bsz].set(c_ctx)
    zero_carry = jnp.zeros((bsz, 2, S5_LANES), F32)

    lw = _layer_weights(w_in, a_q_gain, a_k_gain, c_q_a_gain, c_kv_a_gain, c_w_qb, c_w_kvb)
    sw = _s5_weights(s5_a_re, s5_a_im, s5_log_dt, s5_b_re, s5_b_im, s5_c_re, s5_c_im)
    d_rows = s5_d[:, None, :]
    w_glu = s5_w_glu.astype(BF16)
    merge_w = [lw['w_gate'], w_branch_a.astype(BF16), w_branch_s5.astype(BF16), w_branch_c.astype(BF16),
               w_out.astype(BF16), ln1_g[:, None, :], ln1_b[:, None, :]]
    mlp_w = [w_up.astype(BF16), w_down.astype(BF16), ln2_g[:, None, :], ln2_b[:, None, :]]

    for l in range(DEPTH):
        last = l == DEPTH - 1
        mod = _modulation(c_all, w_mod, b_mod, l).reshape(8, N_MOD, D_MODEL)
        mod_lat, mod_ctx = mod[:bsz], mod[bsz:bsz + 1]

        ka_c, va_c, kc_c, vc_c, u_c, qa_c, qc_c = _inproj(ctx, mod_ctx, lw, l, tabs_lat, rope=False)
        ka, va, kc, vc, u, qa, qc = _inproj(x, mod_lat, lw, l, tabs_lat, rope=True)

        u_c, u = _interleave(u_c), _interleave(u)
        yf_c, carry_f = _s5_scan(u_c, zero_carry, sw, l, u_c, d_rows, w_glu, reverse=False, glu=False)
        yf, _ = _s5_scan(u, carry_f, sw, l, u, d_rows, w_glu, reverse=False, glu=False)
        ys_c, carry_b = _s5_scan(u_c, zero_carry, sw, l, yf_c, d_rows, w_glu, reverse=True, glu=True)
        ys, _ = _s5_scan(u, carry_b, sw, l, yf, d_rows, w_glu, reverse=True, glu=True)
        ys_c, ys = _deinterleave(ys_c), _deinterleave(ys)

        ya = _attention(qa, ka, ka_c, va, va_c, heads=A_HEADS, kv_heads=A_KV_HEADS,
                        dk=A_HEAD_DIM, dv=A_HEAD_DIM, name="attn_a")
        yc = _attention(qc, kc, kc_c, vc, vc_c, heads=C_HEADS, kv_heads=C_HEADS,
                        dk=C_HEAD_PAD, dv=C_VDIM, name="attn_c")
        x_mid = _token_call(_merge_kernel, "merge", x, mod_lat, [ya, ys, yc], merge_w, l, MERGE_ROW_BLOCK)
        x_next = _token_call(_mlp_kernel, "mlp", x_mid, mod_lat, [], mlp_w, l, MLP_ROW_BLOCK)

        if not last:
            ya_c = _attention_ctx(qa_c, ka_c, va_c, heads=A_HEADS, kv_heads=A_KV_HEADS,
                                  dk=A_HEAD_DIM, dv=A_HEAD_DIM, name="attn_a_ctx")
            yc_c = _attention_ctx(qc_c, kc_c, vc_c, heads=C_HEADS, kv_heads=C_HEADS,
                                  dk=C_HEAD_PAD, dv=C_VDIM, name="attn_c_ctx")
            ctx_mid = _token_call(_merge_kernel, "merge_ctx", ctx, mod_ctx, [ya_c, ys_c, yc_c], merge_w, l,
                                  MERGE_ROW_BLOCK)
            ctx = _token_call(_mlp_kernel, "mlp_ctx", ctx_mid, mod_ctx, [], mlp_w, l, MLP_ROW_BLOCK)
        x = x_next
    return x
```
